```python
import math
import jax, jax.numpy as jnp
from jax import lax
import numpy as np

D_MODEL = 1024
BATCH = 2
SEQ = 8192
DEPTH = 4
DEC_BATCH = 128
DEC_SEQ = 8
PAST_LEN = 2048
PAGE_SIZE = 128

N_EVEN = (DEPTH + 1) // 2
N_ODD = DEPTH // 2
SSM_HEADS = 16
SSM_HEAD_DIM = 64
D_SSM = SSM_HEADS * SSM_HEAD_DIM
SSM_GROUPS = 2
SSM_STATE = 128
SSM_CONV = 4
SSD_CHUNK = 128
SSM_CONV_DIM = D_SSM + 2 * SSM_GROUPS * SSM_STATE
D_CONF = 1024
CONF_WIDTH = 31
D_IN_EVEN = D_SSM + SSM_CONV_DIM + SSM_HEADS + 2 * D_CONF
ATTN_HEADS = 16
ATTN_HEAD_DIM = 64
D_ATTN = ATTN_HEADS * ATTN_HEAD_DIM
MOBA_BLOCK = 256
MOBA_TOPK = 3
Q_BLOCK = 128
D_FF = 2816
N_EXPERTS = 8
EXPERT_TOPK = 2
PLE_DIM = 256
DEEPNORM_ALPHA = (2.0 * DEPTH) ** 0.25
DEEPNORM_BETA = (8.0 * DEPTH) ** -0.25
LN_EPS = 1e-5

kernel_name = 'hybrid_ssd_conformer_moba_decode_step'


def alibi_slopes(n):
    return 2.0 ** (-8.0 * jnp.arange(1, n + 1, dtype=jnp.float32) / n)


def layer_norm(x, g, b):
    xf = x.astype(jnp.float32)
    xc = xf - jnp.mean(xf, -1, keepdims=True)
    var = jnp.mean(xc * xc, -1, keepdims=True)
    return (xc * lax.rsqrt(var + LN_EPS) * g.astype(jnp.float32) + b.astype(jnp.float32)).astype(x.dtype)


def rms_norm(x, g):
    xf = x.astype(jnp.float32)
    return xf * lax.rsqrt(jnp.mean(xf * xf, -1, keepdims=True) + LN_EPS) * g.astype(jnp.float32)


def causal_dwconv(buf, u, w, bias):
    width, ch = w.shape
    full = jnp.concatenate([buf.astype(u.dtype), u], axis=1)
    out = lax.conv_general_dilated(full, w.astype(u.dtype)[:, None, :], (1,), 'VALID',
                                   dimension_numbers=('NWC', 'WIO', 'NWC'), feature_group_count=ch)
    return out + bias.astype(u.dtype), full[:, full.shape[1] - (width - 1):]


def ssd_scan(xh, dt, a, bh, ch, s0):
    b, L, H, P = xh.shape
    N = bh.shape[-1]
    Q = SSD_CHUNK if L % SSD_CHUNK == 0 else L
    nc = L // Q
    xc = xh.reshape(b, nc, Q, H, P)
    bc = bh.reshape(b, nc, Q, H, N)
    cc = ch.reshape(b, nc, Q, H, N)
    dtc = dt.reshape(b, nc, Q, H)
    acum = jnp.cumsum(dtc * a, axis=2)
    seg = acum[:, :, :, None, :] - acum[:, :, None, :, :]
    mask = jnp.tril(jnp.ones((Q, Q), bool))[None, None, :, :, None]
    decay = jnp.where(mask, jnp.exp(jnp.where(mask, seg, 0.0)), 0.0)
    scores = jnp.einsum('bcthn,bcshn->bctsh', cc, bc) * decay * dtc[:, :, None, :, :]
    y_diag = jnp.einsum('bctsh,bcshp->bcthp', scores, xc)
    w_end = jnp.exp(acum[:, :, -1:, :] - acum) * dtc
    s_chunk = jnp.einsum('bcsh,bcshn,bcshp->bchpn', w_end, bc, xc)
    chunk_decay = jnp.exp(acum[:, :, -1, :])

    def step(s, inp):
        dec, sc = inp
        return dec[:, :, None, None] * s + sc, s

    final, s_prev = lax.scan(step, s0, (jnp.moveaxis(chunk_decay, 1, 0), jnp.moveaxis(s_chunk, 1, 0)))
    s_prev = jnp.moveaxis(s_prev, 0, 1)
    y_off = jnp.einsum('bcthn,bcth,bchpn->bcthp', cc, jnp.exp(acum), s_prev)
    return (y_diag + y_off).reshape(b, L, H, P), final


def even_mixer(x, conv_buf, ssm_state, conf_buf, w_in, conv_w, conv_b, dt_bias, a_log, d_skip,
               norm_g, dw_w, dw_b, ln_g, ln_b, w_out):
    b, L, _ = x.shape
    proj = x @ w_in
    s1 = D_SSM
    s2 = s1 + SSM_CONV_DIM
    s3 = s2 + SSM_HEADS
    s4 = s3 + D_CONF
    z, xbc, dt_raw, conf_a, conf_g = jnp.split(proj, [s1, s2, s3, s4], axis=-1)
    xbc, new_conv = causal_dwconv(conv_buf, xbc, conv_w, conv_b)
    xbc = jax.nn.silu(xbc)
    xs, bm, cm = jnp.split(xbc, [D_SSM, D_SSM + SSM_GROUPS * SSM_STATE], axis=-1)
    hpg = SSM_HEADS // SSM_GROUPS
    xh = xs.reshape(b, L, SSM_HEADS, SSM_HEAD_DIM).astype(jnp.float32)
    bh = jnp.repeat(bm.reshape(b, L, SSM_GROUPS, SSM_STATE), hpg, axis=2).astype(jnp.float32)
    chh = jnp.repeat(cm.reshape(b, L, SSM_GROUPS, SSM_STATE), hpg, axis=2).astype(jnp.float32)
    dt = jax.nn.softplus(dt_raw.astype(jnp.float32) + dt_bias.astype(jnp.float32))
    a = -jnp.exp(a_log.astype(jnp.float32))
    y, new_state = ssd_scan(xh, dt, a, bh, chh, ssm_state.astype(jnp.float32))
    y = y + d_skip.astype(jnp.float32)[:, None] * xh
    y = rms_norm(y.reshape(b, L, D_SSM) * jax.nn.silu(z.astype(jnp.float32)), norm_g)
    u = conf_a * jax.nn.sigmoid(conf_g)
    u, new_conf = causal_dwconv(conf_buf, u, dw_w, dw_b)
    u = jax.nn.silu(layer_norm(u, ln_g, ln_b))
    out = jnp.concatenate([y.astype(x.dtype), u], axis=-1) @ w_out
    return out, new_state, new_conv, new_conf


def moba_block(q, q_pos, kb, vb, kmean, own_k, own_v, own_pos, n_cand, topk, slopes):
    H = q.shape[1]
    qf = q.astype(jnp.float32)
    d_own = (q_pos[None, :, None] - own_pos[None, None, :]).astype(jnp.float32)
    s_own = jnp.einsum('thd,shd->hts', qf, own_k.astype(jnp.float32)) - slopes[:, None, None] * d_own
    s_own = jnp.where(own_pos[None, None, :] <= q_pos[None, :, None], s_own, -jnp.inf)
    if topk == 0:
        p = jax.nn.softmax(s_own, axis=-1)
        return jnp.einsum('hts,shd->thd', p, own_v.astype(jnp.float32)).astype(q.dtype)
    nb = kb.shape[0]
    gate = jnp.einsum('thd,nhd->htn', qf, kmean)
    gate = jnp.where(jnp.arange(nb)[None, None, :] < n_cand, gate, -jnp.inf)
    gval, gidx = lax.top_k(gate, topk)
    valid = jnp.isfinite(gval)
    hidx = jnp.arange(H)[:, None, None]
    ksel = jnp.transpose(kb, (2, 0, 1, 3))[hidx, gidx]
    vsel = jnp.transpose(vb, (2, 0, 1, 3))[hidx, gidx]
    sel_pos = gidx[..., None] * MOBA_BLOCK + jnp.arange(MOBA_BLOCK, dtype=jnp.int32)
    d_sel = (q_pos[None, :, None, None] - sel_pos).astype(jnp.float32)
    s_sel = jnp.einsum('thd,htkjd->htkj', qf, ksel.astype(jnp.float32)) - slopes[:, None, None, None] * d_sel
    s_sel = jnp.where(valid[..., None], s_sel, -jnp.inf)
    n_sel = topk * MOBA_BLOCK
    p = jax.nn.softmax(jnp.concatenate([s_sel.reshape(H, -1 if False else s_sel.shape[1], n_sel), s_own], axis=-1), axis=-1)
    p_sel = p[..., :n_sel].reshape(s_sel.shape)
    p_own = p[..., n_sel:]
    o = (jnp.einsum('htkj,htkjd->thd', p_sel, vsel.astype(jnp.float32))
         + jnp.einsum('hts,shd->thd', p_own, own_v.astype(jnp.float32)))
    return o.astype(q.dtype)


def split_qkv(x, w_qkv):
    b, L, _ = x.shape
    q, k, v = jnp.split(x @ w_qkv, 3, axis=-1)
    shp = (b, L, ATTN_HEADS, ATTN_HEAD_DIM)
    return q.reshape(shp) * (ATTN_HEAD_DIM ** -0.5), k.reshape(shp), v.reshape(shp)


def moba_prompt(x, w_qkv, w_o, slopes):
    b, L, _ = x.shape
    q, k, v = split_qkv(x, w_qkv)
    nblk = -(-L // MOBA_BLOCK)
    pad = nblk * MOBA_BLOCK - L
    blk_shape = (b, nblk, MOBA_BLOCK, ATTN_HEADS, ATTN_HEAD_DIM)
    kb = jnp.pad(k, ((0, 0), (0, pad), (0, 0), (0, 0))).reshape(blk_shape)
    vb = jnp.pad(v, ((0, 0), (0, pad), (0, 0), (0, 0))).reshape(blk_shape)
    kmean = jnp.mean(kb.astype(jnp.float32), axis=2)
    topk = min(MOBA_TOPK, nblk - 1)
    nqb = L // Q_BLOCK
    b_idx = jnp.repeat(jnp.arange(b, dtype=jnp.int32), nqb)
    q_idx = jnp.tile(jnp.arange(nqb, dtype=jnp.int32), b)

    def body(idx):
        bi, qi = idx
        start = qi * Q_BLOCK
        q_blk = lax.dynamic_slice_in_dim(q[bi], start, Q_BLOCK, axis=0)
        q_pos = start + jnp.arange(Q_BLOCK, dtype=jnp.int32)
        j = start // MOBA_BLOCK
        kb_b = kb[bi]
        vb_b = vb[bi]
        own_k = lax.dynamic_index_in_dim(kb_b, j, 0, keepdims=False)
        own_v = lax.dynamic_index_in_dim(vb_b, j, 0, keepdims=False)
        own_pos = j * MOBA_BLOCK + jnp.arange(MOBA_BLOCK, dtype=jnp.int32)
        return moba_block(q_blk, q_pos, kb_b, vb_b, kmean[bi], own_k, own_v, own_pos, j, topk, slopes)

    o = lax.map(body, (b_idx, q_idx))
    return o.reshape(b, L, D_ATTN) @ w_o, k, v


def moba_sample(x, cache_k, cache_v, page_table, layer, w_qkv, w_o, slopes):
    b, T, _ = x.shape
    q, k, v = split_qkv(x, w_qkv)
    past = page_table.shape[1] * PAGE_SIZE
    nfull = past // MOBA_BLOCK
    r0 = nfull * MOBA_BLOCK
    topk = min(MOBA_TOPK, nfull)
    q_pos = past + jnp.arange(T, dtype=jnp.int32)
    own_pos = jnp.concatenate([jnp.arange(r0, past, dtype=jnp.int32), q_pos])

    def body(args):
        qs, kn, vn, pt = args
        k_past = cache_k[layer, pt].reshape(past, ATTN_HEADS, ATTN_HEAD_DIM).astype(kn.dtype)
        v_past = cache_v[layer, pt].reshape(past, ATTN_HEADS, ATTN_HEAD_DIM).astype(vn.dtype)
        kb = k_past[:r0].reshape(nfull, MOBA_BLOCK, ATTN_HEADS, ATTN_HEAD_DIM)
        vb = v_past[:r0].reshape(nfull, MOBA_BLOCK, ATTN_HEADS, ATTN_HEAD_DIM)
        kmean = jnp.mean(kb.astype(jnp.float32), axis=1)
        own_k = jnp.concatenate([k_past[r0:], kn], axis=0)
        own_v = jnp.concatenate([v_past[r0:], vn], axis=0)
        return moba_block(qs, q_pos, kb, vb, kmean, own_k, own_v, own_pos, nfull, topk, slopes)

    o = lax.map(body, (q, k, v, page_table))
    return o.reshape(b, T, D_ATTN) @ w_o, k, v


def swiglu(x, wg, wu, wd):
    return (jax.nn.silu(x @ wg) * (x @ wu)) @ wd


def moe_ffn(x, w_router, wg, wu, wd):
    logits = (x @ w_router).astype(jnp.float32)
    top_val, top_idx = lax.top_k(logits, EXPERT_TOPK)
    top_w = jax.nn.softmax(top_val, axis=-1)
    gates = jnp.sum(jax.nn.one_hot(top_idx, N_EXPERTS, dtype=jnp.float32) * top_w[..., None], axis=-2)
    out = jnp.zeros_like(x)
    for e in range(N_EXPERTS):
        out = out + gates[..., e:e + 1].astype(x.dtype) * swiglu(x, wg[e], wu[e], wd[e])
    return out


def per_layer_embed(x, p, w_p, w_g):
    gate = jax.nn.sigmoid((x @ w_g).astype(jnp.float32))
    return x + (gate * (p @ w_p).astype(jnp.float32)).astype(x.dtype)


def setup_inputs(seed: int = 0) -> dict:
    key = jax.random.key(seed)
    keys = iter(jax.random.split(key, 64))

    def nrm(shape, scale):
        return jax.random.normal(next(keys), shape, jnp.float32) * scale

    n_pages = PAST_LEN // PAGE_SIZE
    n_pool = (DEC_BATCH * n_pages * 5) // 4
    page_table = jax.random.permutation(next(keys), n_pool)[: DEC_BATCH * n_pages]
    page_table = page_table.reshape(DEC_BATCH, n_pages).astype(jnp.int32)
    dt0 = jnp.exp(jax.random.uniform(next(keys), (N_EVEN, SSM_HEADS), jnp.float32, math.log(1e-3), math.log(1e-1)))
    a0 = jax.random.uniform(next(keys), (N_EVEN, SSM_HEADS), jnp.float32, 1.0, 16.0)
    D = D_MODEL
    return {
        'x_prompt': nrm((BATCH, SEQ, D), 1.0),
        'x_sample': nrm((DEC_BATCH, DEC_SEQ, D), 1.0),
        'p_prompt': nrm((DEPTH, BATCH, SEQ, PLE_DIM), 1.0),
        'p_sample': nrm((DEPTH, DEC_BATCH, DEC_SEQ, PLE_DIM), 1.0),
        'state_ssm': nrm((N_EVEN, DEC_BATCH, SSM_HEADS, SSM_HEAD_DIM, SSM_STATE), 0.5),
        'state_ssm_conv': nrm((N_EVEN, DEC_BATCH, SSM_CONV - 1, SSM_CONV_DIM), 1.0),
        'state_conf_conv': nrm((N_EVEN, DEC_BATCH, CONF_WIDTH - 1, D_CONF), 1.0),
        'cache_k': nrm((N_ODD, n_pool, PAGE_SIZE, ATTN_HEADS, ATTN_HEAD_DIM), 1.0),
        'cache_v': nrm((N_ODD, n_pool, PAGE_SIZE, ATTN_HEADS, ATTN_HEAD_DIM), 1.0),
        'page_table': page_table,
        'w_in_even': nrm((N_EVEN, D, D_IN_EVEN), D ** -0.5),
        'ssm_conv_w': nrm((N_EVEN, SSM_CONV, SSM_CONV_DIM), SSM_CONV ** -0.5),
        'ssm_conv_b': nrm((N_EVEN, SSM_CONV_DIM), 0.02),
        'ssm_dt_bias': dt0 + jnp.log(-jnp.expm1(-dt0)),
        'ssm_a_log': jnp.log(a0),
        'ssm_d': 1.0 + nrm((N_EVEN, SSM_HEADS), 0.1),
        'ssm_norm_g': 1.0 + nrm((N_EVEN, D_SSM), 0.01),
        'conf_dw_w': nrm((N_EVEN, CONF_WIDTH, D_CONF), CONF_WIDTH ** -0.5),
        'conf_dw_b': nrm((N_EVEN, D_CONF), 0.02),
        'conf_ln_g': 1.0 + nrm((N_EVEN, D_CONF), 0.01),
        'conf_ln_b': nrm((N_EVEN, D_CONF), 0.01),
        'w_out_even': nrm((N_EVEN, D_SSM + D_CONF, D), (D_SSM + D_CONF) ** -0.5 * DEEPNORM_BETA),
        'w_qkv': nrm((N_ODD, D, 3 * D_ATTN), D ** -0.5),
        'w_o': nrm((N_ODD, D_ATTN, D), D_ATTN ** -0.5 * DEEPNORM_BETA),
        'ln_mix_g': 1.0 + nrm((DEPTH, D), 0.01),
        'ln_mix_b': nrm((DEPTH, D), 0.01),
        'ln_ffn_g': 1.0 + nrm((DEPTH, D), 0.01),
        'ln_ffn_b': nrm((DEPTH, D), 0.01),
        'w_ffn_gate': nrm((N_EVEN, D, D_FF), D ** -0.5),
        'w_ffn_up': nrm((N_EVEN, D, D_FF), D ** -0.5),
        'w_ffn_down': nrm((N_EVEN, D_FF, D), D_FF ** -0.5 * DEEPNORM_BETA),
        'w_router': nrm((N_ODD, D, N_EXPERTS), D ** -0.5),
        'w_exp_gate': nrm((N_ODD, N_EXPERTS, D, D_FF), D ** -0.5),
        'w_exp_up': nrm((N_ODD, N_EXPERTS, D, D_FF), D ** -0.5),
        'w_exp_down': nrm((N_ODD, N_EXPERTS, D_FF, D), D_FF ** -0.5 * DEEPNORM_BETA),
        'w_ple': nrm((DEPTH, PLE_DIM, D), PLE_DIM ** -0.5),
        'w_ple_gate': nrm((DEPTH, D, D), D ** -0.5),
    }


def reference(x_prompt, x_sample, p_prompt, p_sample, state_ssm, state_ssm_conv, state_conf_conv,
              cache_k, cache_v, page_table,
              w_in_even, ssm_conv_w, ssm_conv_b, ssm_dt_bias, ssm_a_log, ssm_d, ssm_norm_g,
              conf_dw_w, conf_dw_b, conf_ln_g, conf_ln_b, w_out_even,
              w_qkv, w_o, ln_mix_g, ln_mix_b, ln_ffn_g, ln_ffn_b,
              w_ffn_gate, w_ffn_up, w_ffn_down, w_router, w_exp_gate, w_exp_up, w_exp_down,
              w_ple, w_ple_gate):
    slopes = alibi_slopes(ATTN_HEADS)
    xp, xs = x_prompt, x_sample
    bp = xp.shape[0]
    ssm_p, sconv_p, conf_p, k_p, v_p = [], [], [], [], []
    ssm_s, sconv_s, conf_s, k_s, v_s = [], [], [], [], []
    for i in range(DEPTH):
        li = i // 2
        if i % 2 == 0:
            ew = (w_in_even[li], ssm_conv_w[li], ssm_conv_b[li], ssm_dt_bias[li], ssm_a_log[li], ssm_d[li],
                  ssm_norm_g[li], conf_dw_w[li], conf_dw_b[li], conf_ln_g[li], conf_ln_b[li], w_out_even[li])
            zc = jnp.zeros((bp, SSM_CONV - 1, SSM_CONV_DIM), xp.dtype)
            zs = jnp.zeros((bp, SSM_HEADS, SSM_HEAD_DIM, SSM_STATE), jnp.float32)
            zf = jnp.zeros((bp, CONF_WIDTH - 1, D_CONF), xp.dtype)
            hp, st_p, cv_p, cf_p = even_mixer(xp, zc, zs, zf, *ew)
            hs, st_s, cv_s, cf_s = even_mixer(xs, state_ssm_conv[li], state_ssm[li], state_conf_conv[li], *ew)
            ssm_p.append(st_p); sconv_p.append(cv_p); conf_p.append(cf_p)
            ssm_s.append(st_s); sconv_s.append(cv_s); conf_s.append(cf_s)
        else:
            hp, kp_, vp_ = moba_prompt(xp, w_qkv[li], w_o[li], slopes)
            hs, ks_, vs_ = moba_sample(xs, cache_k, cache_v, page_table, li, w_qkv[li], w_o[li], slopes)
            k_p.append(kp_); v_p.append(vp_); k_s.append(ks_); v_s.append(vs_)
        xp = layer_norm(DEEPNORM_ALPHA * xp + hp, ln_mix_g[i], ln_mix_b[i])
        xs = layer_norm(DEEPNORM_ALPHA * xs + hs, ln_mix_g[i], ln_mix_b[i])
        if i % 2 == 0:
            fp = swiglu(xp, w_ffn_gate[li], w_ffn_up[li], w_ffn_down[li])
            fs = swiglu(xs, w_ffn_gate[li], w_ffn_up[li], w_ffn_down[li])
        else:
            fp = moe_ffn(xp, w_router[li], w_exp_gate[li], w_exp_up[li], w_exp_down[li])
            fs = moe_ffn(xs, w_router[li], w_exp_gate[li], w_exp_up[li], w_exp_down[li])
        xp = layer_norm(DEEPNORM_ALPHA * xp + fp, ln_ffn_g[i], ln_ffn_b[i])
        xs = layer_norm(DEEPNORM_ALPHA * xs + fs, ln_ffn_g[i], ln_ffn_b[i])
        xp = per_layer_embed(xp, p_prompt[i], w_ple[i], w_ple_gate[i])
        xs = per_layer_embed(xs, p_sample[i], w_ple[i], w_ple_gate[i])
    return (xp, xs,
            jnp.stack(ssm_p), jnp.stack(sconv_p), jnp.stack(conf_p), jnp.stack(k_p), jnp.stack(v_p),
            jnp.stack(ssm_s), jnp.stack(sconv_s), jnp.stack(conf_s), jnp.stack(k_s), jnp.stack(v_s))
```

```python
import functools
import math

import jax
import jax.numpy as jnp
from jax import lax
from jax.experimental import pallas as pl
from jax.experimental.pallas import tpu as pltpu

F32 = jnp.float32
BF16 = jnp.bfloat16
HI = lax.Precision.HIGHEST

D_MODEL = 1024
DEPTH = 4
PAGE_SIZE = 128
SSM_HEADS = 16
SSM_HEAD_DIM = 64
D_SSM = SSM_HEADS * SSM_HEAD_DIM
SSM_GROUPS = 2
SSM_STATE = 128
SSM_CONV = 4
SSD_CHUNK = 128
SSM_CONV_DIM = D_SSM + 2 * SSM_GROUPS * SSM_STATE
D_CONF = 1024
CONF_WIDTH = 31
ATTN_HEADS = 16
ATTN_HEAD_DIM = 64
D_ATTN = ATTN_HEADS * ATTN_HEAD_DIM
MOBA_BLOCK = 256
MOBA_TOPK = 3
Q_BLOCK = 128
D_FF = 2816
N_EXPERTS = 8
EXPERT_TOPK = 2
PLE_DIM = 256
DEEPNORM_ALPHA = (2.0 * DEPTH) ** 0.25
LN_EPS = 1e-5

LANES = 128
NEG_BIG = -1e30
VMEM_LIMIT = 56 * 1024 * 1024
N_PAIRS = ATTN_HEADS // 2
FEAT_SEL0 = 4


def _cparams(sem):
    return pltpu.CompilerParams(dimension_semantics=sem, vmem_limit_bytes=VMEM_LIMIT)


def _silu(x):
    return x * jax.nn.sigmoid(x)


def _layer_norm(x, g, b):
    xc = x - jnp.mean(x, -1, keepdims=True)
    var = jnp.mean(xc * xc, -1, keepdims=True)
    return xc * lax.rsqrt(var + LN_EPS) * g + b


def _dot(a, b):
    return jnp.dot(a, b, preferred_element_type=F32)


def _dot_nt(a, b, precision=None):
    return lax.dot_general(a, b, (((1,), (1,)), ((), ())), precision=precision,
                           preferred_element_type=F32)


def _dot_tn(a, b):
    return lax.dot_general(a, b, (((0,), (0,)), ((), ())), preferred_element_type=F32)


ZX_W = D_SSM + SSM_CONV_DIM


def _proj_even_kernel(x_ref, w_ref, zx_ref, u_ref, dt_ref):
    x = x_ref[...].astype(BF16)
    zx_ref[...] = _dot(x, w_ref[:, 0:ZX_W])
    a = _dot(x, w_ref[:, ZX_W:ZX_W + D_CONF])
    g = _dot(x, w_ref[:, ZX_W + D_CONF:ZX_W + 2 * D_CONF])
    u_ref[...] = a * jax.nn.sigmoid(g)
    dt_ref[...] = _dot(x, w_ref[:, ZX_W + 2 * D_CONF:])


def proj_even(x, w, tm=256):
    m = x.shape[0]
    tm = min(tm, m)
    assert m % tm == 0
    n = w.shape[1]
    return pl.pallas_call(
        _proj_even_kernel,
        grid=(m // tm,),
        in_specs=[pl.BlockSpec((tm, D_MODEL), lambda i: (i, 0)),
                  pl.BlockSpec((D_MODEL, n), lambda i: (0, 0))],
        out_specs=[pl.BlockSpec((tm, ZX_W), lambda i: (i, 0)),
                   pl.BlockSpec((tm, D_CONF), lambda i: (i, 0)),
                   pl.BlockSpec((tm, LANES), lambda i: (i, 0))],
        out_shape=[jax.ShapeDtypeStruct((m, ZX_W), F32),
                   jax.ShapeDtypeStruct((m, D_CONF), F32),
                   jax.ShapeDtypeStruct((m, LANES), F32)],
        compiler_params=_cparams(("arbitrary",)),
        name="proj_even",
    )(x, w)


SSM_PAD = 8
SSM_OFF = SSM_PAD - (SSM_CONV - 1)


def _ssd_kernel(zx_ref, dtr_ref, cbuf_ref, s0_ref, cw_ref, cb_ref, dtb_ref, alog_ref, dsk_ref, ng_ref,
                y_ref, sout_ref, cout_ref, ext_ref, st_ref, ybuf_ref, *, q, nc):
    c = pl.program_id(1)

    @pl.when(c == 0)
    def _():
        ext_ref[0:SSM_PAD, :] = jnp.zeros((SSM_PAD, SSM_CONV_DIM), F32)
        ext_ref[SSM_OFF:SSM_PAD, :] = cbuf_ref[0]
        st_ref[...] = s0_ref[0]

    ext_ref[SSM_PAD:SSM_PAD + q, :] = zx_ref[0, :, D_SSM:ZX_W]
    acc = jnp.broadcast_to(cb_ref[...], (q, SSM_CONV_DIM))
    for k in range(SSM_CONV):
        acc = acc + ext_ref[SSM_OFF + k:SSM_OFF + k + q, :] * cw_ref[k:k + 1, :]
    xbc = _silu(acc)

    @pl.when(c == nc - 1)
    def _():
        cout_ref[0] = ext_ref[q + SSM_OFF:q + SSM_PAD, :]

    ext_ref[0:SSM_PAD, :] = ext_ref[q:q + SSM_PAD, :]

    bm = xbc[:, D_SSM:D_SSM + SSM_GROUPS * SSM_STATE]
    cm = xbc[:, D_SSM + SSM_GROUPS * SSM_STATE:]
    xdt = dtr_ref[0] + dtb_ref[...]
    dt = jnp.maximum(xdt, 0.0) + jnp.log(1.0 + jnp.exp(-jnp.abs(xdt)))
    a = -jnp.exp(alog_ref[...])
    dta = dt * a
    rows = lax.broadcasted_iota(jnp.int32, (q, q), 0)
    cols = lax.broadcasted_iota(jnp.int32, (q, q), 1)
    tril = rows >= cols
    acum = jnp.dot(tril.astype(F32), dta, precision=HI, preferred_element_type=F32)
    acum_t = acum.T
    dt_t = dt.T
    wend = jnp.exp(acum[q - 1:q, :] - acum) * dt
    ea = jnp.exp(acum)
    dec_t = jnp.exp(acum_t[:, q - 1:q])

    bmb = bm.astype(BF16)
    cmb = cm.astype(BF16)
    cb = [_dot_nt(cmb[:, g * SSM_STATE:(g + 1) * SSM_STATE], bmb[:, g * SSM_STATE:(g + 1) * SSM_STATE])
          for g in range(SSM_GROUPS)]
    lane = lax.broadcasted_iota(jnp.int32, (q, LANES), 1)
    is_a = lane < SSM_HEAD_DIM
    sub = lax.broadcasted_iota(jnp.int32, (LANES, LANES), 0)
    hpg = SSM_HEADS // SSM_GROUPS

    def scores(h, g):
        seg = acum[:, h:h + 1] - acum_t[h:h + 1, :]
        dec = jnp.where(tril, jnp.exp(jnp.where(tril, seg, 0.0)), 0.0)
        return (cb[g] * dec * dt_t[h:h + 1, :]).astype(BF16)

    for p in range(SSM_HEADS // 2):
        ha, hb = 2 * p, 2 * p + 1
        g = ha // hpg
        xp = xbc[:, p * LANES:(p + 1) * LANES]
        xa = jnp.where(is_a, xp, 0.0).astype(BF16)
        xb = jnp.where(is_a, 0.0, xp).astype(BF16)
        y = _dot(scores(ha, g), xa) + _dot(scores(hb, g), xb)
        sp = st_ref[p]
        ea_p = jnp.where(is_a, ea[:, ha:ha + 1], ea[:, hb:hb + 1])
        y = y + ea_p * _dot_nt(cmb[:, g * SSM_STATE:(g + 1) * SSM_STATE], sp.astype(BF16))
        y = y + dsk_ref[:, p * LANES:(p + 1) * LANES] * xp
        ybuf_ref[:, p * LANES:(p + 1) * LANES] = y
        wend_p = jnp.where(is_a, wend[:, ha:ha + 1], wend[:, hb:hb + 1])
        upd = _dot_tn((xp * wend_p).astype(BF16), bmb[:, g * SSM_STATE:(g + 1) * SSM_STATE])
        dec_p = jnp.where(sub < SSM_HEAD_DIM, dec_t[ha:ha + 1, :], dec_t[hb:hb + 1, :])
        st_ref[p] = dec_p * sp + upd

    z = zx_ref[0, :, 0:D_SSM]
    yy = ybuf_ref[...] * _silu(z)
    y_ref[0] = yy * lax.rsqrt(jnp.mean(yy * yy, -1, keepdims=True) + LN_EPS) * ng_ref[...]

    @pl.when(c == nc - 1)
    def _():
        sout_ref[0] = st_ref[...]


def ssd_mixer(zx, dtr, cbuf, s0, cw, cb, dtb, alog, dsk, ng):
    b, L, _ = zx.shape
    q = SSD_CHUNK if L % SSD_CHUNK == 0 else L
    nc = L // q
    npair = SSM_HEADS // 2
    s0p = s0.reshape(b, npair, 2 * SSM_HEAD_DIM, SSM_STATE)
    const = lambda shape: pl.BlockSpec(shape, lambda i, j: (0,) * len(shape))
    y, sout, cout = pl.pallas_call(
        functools.partial(_ssd_kernel, q=q, nc=nc),
        grid=(b, nc),
        in_specs=[pl.BlockSpec((1, q, ZX_W), lambda i, j: (i, j, 0)),
                  pl.BlockSpec((1, q, LANES), lambda i, j: (i, j, 0)),
                  pl.BlockSpec((1, SSM_CONV - 1, SSM_CONV_DIM), lambda i, j: (i, 0, 0)),
                  pl.BlockSpec((1, npair, 2 * SSM_HEAD_DIM, SSM_STATE), lambda i, j: (i, 0, 0, 0)),
                  const((SSM_CONV, SSM_CONV_DIM)), const((1, SSM_CONV_DIM)),
                  const((1, LANES)), const((1, LANES)), const((1, D_SSM)), const((1, D_SSM))],
        out_specs=[pl.BlockSpec((1, q, D_SSM), lambda i, j: (i, j, 0)),
                   pl.BlockSpec((1, npair, 2 * SSM_HEAD_DIM, SSM_STATE), lambda i, j: (i, 0, 0, 0)),
                   pl.BlockSpec((1, SSM_CONV - 1, SSM_CONV_DIM), lambda i, j: (i, 0, 0))],
        out_shape=[jax.ShapeDtypeStruct((b, L, D_SSM), F32),
                   jax.ShapeDtypeStruct((b, npair, 2 * SSM_HEAD_DIM, SSM_STATE), F32),
                   jax.ShapeDtypeStruct((b, SSM_CONV - 1, SSM_CONV_DIM), F32)],
        scratch_shapes=[pltpu.VMEM((q + SSM_PAD, SSM_CONV_DIM), F32),
                        pltpu.VMEM((npair, 2 * SSM_HEAD_DIM, SSM_STATE), F32),
                        pltpu.VMEM((q, D_SSM), F32)],
        compiler_params=_cparams(("arbitrary", "arbitrary")),
        name="ssd_mixer",
    )(zx, dtr, cbuf, s0p, cw, cb, dtb, alog, dsk, ng)
    return y, sout.reshape(b, SSM_HEADS, SSM_HEAD_DIM, SSM_STATE), cout


CONF_PAD = 32
CONF_OFF = CONF_PAD - (CONF_WIDTH - 1)
CONF_ROWS = 32


def _conf_kernel(u_ref, buf_ref, w_ref, b_ref, g_ref, beta_ref, o_ref, bout_ref, ext_ref, acc_ref, *, t, nt):
    i = pl.program_id(1)

    @pl.when(i == 0)
    def _():
        ext_ref[0:CONF_PAD, :] = jnp.zeros((CONF_PAD, D_CONF), F32)
        ext_ref[CONF_OFF:CONF_PAD, :] = buf_ref[0]

    ext_ref[CONF_PAD:CONF_PAD + t, :] = u_ref[0]
    rr = min(CONF_ROWS, t)

    def chan_body(ci, carry):
        c0 = pl.multiple_of(ci * LANES, LANES)
        for r0 in range(0, t, rr):
            acc = jnp.broadcast_to(b_ref[:, pl.ds(c0, LANES)], (rr, LANES))
            for k in range(CONF_WIDTH):
                acc = acc + ext_ref[r0 + CONF_OFF + k:r0 + CONF_OFF + k + rr, pl.ds(c0, LANES)] * w_ref[k:k + 1, pl.ds(c0, LANES)]
            acc_ref[r0:r0 + rr, pl.ds(c0, LANES)] = acc
        return carry

    lax.fori_loop(0, D_CONF // LANES, chan_body, 0)
    o_ref[0] = _silu(_layer_norm(acc_ref[...], g_ref[...], beta_ref[...]))

    @pl.when(i == nt - 1)
    def _():
        bout_ref[0] = ext_ref[t + CONF_OFF:t + CONF_PAD, :]

    if nt > 1:
        ext_ref[0:CONF_PAD, :] = ext_ref[t:t + CONF_PAD, :]


def conf_mixer(u, buf, w, bias, g, beta, t_max=128):
    b, L, _ = u.shape
    t = t_max if L % t_max == 0 else L
    nt = L // t
    assert nt == 1 or t >= CONF_PAD
    const = lambda shape: pl.BlockSpec(shape, lambda i, j: (0,) * len(shape))
    return pl.pallas_call(
        functools.partial(_conf_kernel, t=t, nt=nt),
        grid=(b, nt),
        in_specs=[pl.BlockSpec((1, t, D_CONF), lambda i, j: (i, j, 0)),
                  pl.BlockSpec((1, CONF_WIDTH - 1, D_CONF), lambda i, j: (i, 0, 0)),
                  const((CONF_WIDTH, D_CONF)), const((1, D_CONF)), const((1, D_CONF)), const((1, D_CONF))],
        out_specs=[pl.BlockSpec((1, t, D_CONF), lambda i, j: (i, j, 0)),
                   pl.BlockSpec((1, CONF_WIDTH - 1, D_CONF), lambda i, j: (i, 0, 0))],
        out_shape=[jax.ShapeDtypeStruct((b, L, D_CONF), F32),
                   jax.ShapeDtypeStruct((b, CONF_WIDTH - 1, D_CONF), F32)],
        scratch_shapes=[pltpu.VMEM((t + CONF_PAD, D_CONF), F32), pltpu.VMEM((t, D_CONF), F32)],
        compiler_params=_cparams(("arbitrary", "arbitrary")),
        name="conf_mixer",
    )(u, buf, w, bias, g, beta)


def _out_ln_kernel(*refs, n_in):
    x_ref = refs[0]
    h_refs = refs[1:1 + n_in]
    w_refs = refs[1 + n_in:1 + 2 * n_in]
    g_ref, b_ref, o_ref = refs[1 + 2 * n_in:]
    acc = DEEPNORM_ALPHA * x_ref[...]
    for h_ref, w_ref in zip(h_refs, w_refs):
        acc = acc + _dot(h_ref[...].astype(BF16), w_ref[...])
    o_ref[...] = _layer_norm(acc, g_ref[...], b_ref[...])


def out_proj_ln(x, hs, ws, g, b, tm=512):
    m = x.shape[0]
    tm = min(tm, m)
    assert m % tm == 0
    n_in = len(hs)
    row = lambda k: pl.BlockSpec((tm, k), lambda i: (i, 0))
    const = lambda shape: pl.BlockSpec(shape, lambda i: (0,) * len(shape))
    return pl.pallas_call(
        functools.partial(_out_ln_kernel, n_in=n_in),
        grid=(m // tm,),
        in_specs=[row(D_MODEL)] + [row(h.shape[1]) for h in hs] + [const(w.shape) for w in ws]
                 + [const((1, D_MODEL)), const((1, D_MODEL))],
        out_specs=row(D_MODEL),
        out_shape=jax.ShapeDtypeStruct((m, D_MODEL), F32),
        compiler_params=_cparams(("arbitrary",)),
        name="out_proj_ln",
    )(x, *hs, *ws, g, b)


def _ffn_kernel(x_ref, wr_ref, wg_ref, wu_ref, wd_ref, g_ref, b_ref, o_ref, acc_ref, gates_ref, xb_ref,
                *, routed, ne, nf):
    e = pl.program_id(1)
    f = pl.program_id(2)

    @pl.when((e == 0) & (f == 0))
    def _():
        x = x_ref[...]
        acc_ref[...] = jnp.zeros_like(acc_ref)
        xb_ref[...] = x.astype(BF16)
        if routed:
            tm = x.shape[0]
            lane = lax.broadcasted_iota(jnp.int32, (tm, LANES), 1)
            logits = jnp.dot(x, wr_ref[...], precision=HI, preferred_element_type=F32)
            logits = jnp.where(lane < ne, logits, -jnp.inf)
            v1 = jnp.max(logits, -1, keepdims=True)
            i1 = jnp.min(jnp.where(logits == v1, lane, LANES), -1, keepdims=True)
            rest = jnp.where(lane == i1, -jnp.inf, logits)
            v2 = jnp.max(rest, -1, keepdims=True)
            i2 = jnp.min(jnp.where(rest == v2, lane, LANES), -1, keepdims=True)
            e2 = jnp.exp(v2 - v1)
            w1 = 1.0 / (1.0 + e2)
            gates_ref[...] = jnp.where(lane == i1, w1, 0.0) + jnp.where(lane == i2, e2 * w1, 0.0)

    xb = xb_ref[...]
    h = _silu(_dot(xb, wg_ref[0])) * _dot(xb, wu_ref[0])
    if routed:
        lane = lax.broadcasted_iota(jnp.int32, gates_ref.shape, 1)
        h = h * jnp.sum(jnp.where(lane == e, gates_ref[...], 0.0), -1, keepdims=True)
    acc_ref[...] += _dot(h.astype(BF16), wd_ref[0])

    @pl.when((e == ne - 1) & (f == nf - 1))
    def _():
        o_ref[...] = _layer_norm(DEEPNORM_ALPHA * x_ref[...] + acc_ref[...], g_ref[...], b_ref[...])


def ffn_ln(x, w_router, wg, wu, wd, g, b, routed, tm=512, tf=1408):
    m = x.shape[0]
    tm = min(tm, m)
    assert m % tm == 0
    ne = wg.shape[0]
    nf = D_FF // tf
    const = lambda shape: pl.BlockSpec(shape, lambda i, e, f: (0,) * len(shape))
    return pl.pallas_call(
        functools.partial(_ffn_kernel, routed=routed, ne=ne, nf=nf),
        grid=(m // tm, ne, nf),
        in_specs=[pl.BlockSpec((tm, D_MODEL), lambda i, e, f: (i, 0)),
                  const((D_MODEL, LANES)),
                  pl.BlockSpec((1, D_MODEL, tf), lambda i, e, f: (e, 0, f)),
                  pl.BlockSpec((1, D_MODEL, tf), lambda i, e, f: (e, 0, f)),
                  pl.BlockSpec((1, tf, D_MODEL), lambda i, e, f: (e, f, 0)),
                  const((1, D_MODEL)), const((1, D_MODEL))],
        out_specs=pl.BlockSpec((tm, D_MODEL), lambda i, e, f: (i, 0)),
        out_shape=jax.ShapeDtypeStruct((m, D_MODEL), F32),
        scratch_shapes=[pltpu.VMEM((tm, D_MODEL), F32), pltpu.VMEM((tm, LANES), F32),
                        pltpu.VMEM((tm, D_MODEL), BF16)],
        compiler_params=_cparams(("arbitrary", "arbitrary", "arbitrary")),
        name="ffn_ln",
    )(x, w_router, wg, wu, wd, g, b)


def _ple_kernel(x_ref, p_ref, wg_ref, wp_ref, o_ref):
    x = x_ref[...]
    gate = jax.nn.sigmoid(_dot(x.astype(BF16), wg_ref[...]))
    o_ref[...] = x + gate * _dot(p_ref[...].astype(BF16), wp_ref[...])


def ple(x, p, wg, wp, tm=512):
    m = x.shape[0]
    tm = min(tm, m)
    assert m % tm == 0
    return pl.pallas_call(
        _ple_kernel,
        grid=(m // tm,),
        in_specs=[pl.BlockSpec((tm, D_MODEL), lambda i: (i, 0)),
                  pl.BlockSpec((tm, PLE_DIM), lambda i: (i, 0)),
                  pl.BlockSpec((D_MODEL, D_MODEL), lambda i: (0, 0)),
                  pl.BlockSpec((PLE_DIM, D_MODEL), lambda i: (0, 0))],
        out_specs=pl.BlockSpec((tm, D_MODEL), lambda i: (i, 0)),
        out_shape=jax.ShapeDtypeStruct((m, D_MODEL), F32),
        compiler_params=_cparams(("arbitrary",)),
        name="ple",
    )(x, p, wg, wp)


def _qkv_kernel(x_ref, w_ref, q_ref, k_ref, v_ref, k16_ref, v16_ref, km_ref, *, tm):
    x = x_ref[...].astype(BF16)
    q_ref[...] = _dot(x, w_ref[:, 0:D_ATTN]) * (ATTN_HEAD_DIM ** -0.5)
    k = _dot(x, w_ref[:, D_ATTN:2 * D_ATTN])
    v = _dot(x, w_ref[:, 2 * D_ATTN:])
    k_ref[...] = k
    v_ref[...] = v
    k16_ref[...] = k.astype(BF16)
    v16_ref[...] = v.astype(BF16)
    for j in range(tm // MOBA_BLOCK):
        km_ref[0, j:j + 1, :] = jnp.mean(k[j * MOBA_BLOCK:(j + 1) * MOBA_BLOCK], 0, keepdims=True)


def qkv_proj(x, w, tm=512):
    m = x.shape[0]
    tm = min(tm, m)
    assert m % tm == 0
    assert tm % MOBA_BLOCK == 0
    nb = tm // MOBA_BLOCK
    row = lambda: pl.BlockSpec((tm, D_ATTN), lambda i: (i, 0))
    return pl.pallas_call(
        functools.partial(_qkv_kernel, tm=tm),
        grid=(m // tm,),
        in_specs=[pl.BlockSpec((tm, D_MODEL), lambda i: (i, 0)),
                  pl.BlockSpec((D_MODEL, 3 * D_ATTN), lambda i: (0, 0))],
        out_specs=[row(), row(), row(), row(), row(), pl.BlockSpec((1, nb, D_ATTN), lambda i: (i, 0, 0))],
        out_shape=[jax.ShapeDtypeStruct((m, D_ATTN), F32)] * 3 + [jax.ShapeDtypeStruct((m, D_ATTN), BF16)] * 2
                  + [jax.ShapeDtypeStruct((m // tm, nb, D_ATTN), F32)],
        compiler_params=_cparams(("arbitrary",)),
        name="qkv_proj",
    )(x, w)


def _select_topk(gate, nidx, n_cand):
    g = jnp.where((nidx >= 0) & (nidx < n_cand), gate, -jnp.inf)
    sel = jnp.zeros(gate.shape, jnp.bool_)
    big = jnp.int32(1 << 20)
    for _ in range(MOBA_TOPK):
        m = jnp.max(g, -1, keepdims=True)
        hit = (g == m) & (m > -jnp.inf)
        first = jnp.min(jnp.where(hit, nidx, big), -1, keepdims=True)
        pick = hit & (nidx == first)
        sel = sel | pick
        g = jnp.where(pick, -jnp.inf, g)
    return sel


def _alibi_slopes_col(rows_per_head, n_rows):
    h = lax.broadcasted_iota(jnp.int32, (n_rows, 1), 0) // rows_per_head
    return jnp.exp2(-8.0 * (h + 1).astype(F32) / ATTN_HEADS)


def _moba_prompt_kernel(q_ref, k_ref, v_ref, km_ref, ft_ref, o_ref, *, nblk):
    p = pl.program_id(1)
    qt = pl.program_id(2)
    tq = Q_BLOCK
    j = (qt * tq) // MOBA_BLOCK
    off = qt * tq - j * MOBA_BLOCK
    q = q_ref[0]
    lane = lax.broadcasted_iota(jnp.int32, (tq, LANES), 1)
    is_a = lane < ATTN_HEAD_DIM
    q2 = jnp.concatenate([jnp.where(is_a, q, 0.0), jnp.where(is_a, 0.0, q)], axis=0)
    gate = _dot_nt(q2, km_ref[0], precision=HI)
    lane2 = lax.broadcasted_iota(jnp.int32, (2 * tq, LANES), 1)
    row2 = lax.broadcasted_iota(jnp.int32, (2 * tq, LANES), 0)
    nidx = lane2 - FEAT_SEL0
    sel = _select_topk(gate, nidx, j)
    head = 2 * p + row2 // tq
    slope = jnp.exp2(-8.0 * (head + 1).astype(F32) / ATTN_HEADS)
    s_hi = slope.astype(BF16).astype(F32)
    s_lo = slope - s_hi
    feat = jnp.where(sel | (nidx == j), 0.0, NEG_BIG)
    feat = jnp.where(lane2 < 2, s_hi, jnp.where(lane2 < FEAT_SEL0, s_lo, feat))
    qf = jnp.concatenate([q2, feat], axis=1).astype(BF16)

    def block_scores(n):
        start = pl.multiple_of(n * MOBA_BLOCK, MOBA_BLOCK)
        kf = jnp.concatenate([k_ref[0, pl.ds(start, MOBA_BLOCK), :], ft_ref[n]], axis=1)
        return _dot_nt(qf, kf), v_ref[0, pl.ds(start, MOBA_BLOCK), :]

    s, vb = block_scores(j)
    t_row = lax.broadcasted_iota(jnp.int32, (2 * tq, MOBA_BLOCK), 0) % tq
    key_i = lax.broadcasted_iota(jnp.int32, (2 * tq, MOBA_BLOCK), 1)
    s = jnp.where(key_i <= t_row + off, s, NEG_BIG)
    m0 = jnp.max(s, -1, keepdims=True)
    p0 = jnp.exp(s - m0)
    l0 = jnp.sum(p0, -1, keepdims=True)
    acc0 = _dot(p0.astype(BF16), vb)

    def body(n, carry):
        m, l, acc = carry
        s, vb = block_scores(n)
        m_new = jnp.maximum(m, jnp.max(s, -1, keepdims=True))
        alpha = jnp.exp(m - m_new)
        pn = jnp.exp(s - m_new)
        return (m_new, alpha * l + jnp.sum(pn, -1, keepdims=True), alpha * acc + _dot(pn.astype(BF16), vb))

    m, l, acc = lax.fori_loop(0, j, body, (m0, l0, acc0))
    o = acc / l
    o_ref[0] = jnp.where(is_a, o[0:tq], o[tq:])


def moba_prompt_attn(q, k16, v16, kmean, b, L):
    assert L % MOBA_BLOCK == 0 and MOBA_BLOCK % Q_BLOCK == 0
    nblk = L // MOBA_BLOCK
    assert FEAT_SEL0 + nblk <= LANES
    q3 = q.reshape(b, L, D_ATTN)
    k3 = k16.reshape(b, L, D_ATTN)
    v3 = v16.reshape(b, L, D_ATTN)
    km = jnp.pad(kmean, ((0, 0), (FEAT_SEL0, LANES - FEAT_SEL0 - nblk), (0, 0)))
    n_i = jnp.arange(nblk, dtype=jnp.int32)[:, None, None]
    key_i = jnp.arange(MOBA_BLOCK, dtype=jnp.int32)[None, :, None]
    ln = jnp.arange(LANES, dtype=jnp.int32)[None, None, :]
    ft = jnp.where((ln == 0) | (ln == 2), n_i * MOBA_BLOCK,
                   jnp.where((ln == 1) | (ln == 3), key_i, (ln == n_i + FEAT_SEL0).astype(jnp.int32)))
    ft = ft.astype(BF16)
    o = pl.pallas_call(
        functools.partial(_moba_prompt_kernel, nblk=nblk),
        grid=(b, N_PAIRS, L // Q_BLOCK),
        in_specs=[pl.BlockSpec((1, Q_BLOCK, LANES), lambda i, p, t: (i, t, p)),
                  pl.BlockSpec((1, L, LANES), lambda i, p, t: (i, 0, p)),
                  pl.BlockSpec((1, L, LANES), lambda i, p, t: (i, 0, p)),
                  pl.BlockSpec((1, LANES, LANES), lambda i, p, t: (i, 0, p)),
                  pl.BlockSpec((nblk, MOBA_BLOCK, LANES), lambda i, p, t: (0, 0, 0))],
        out_specs=pl.BlockSpec((1, Q_BLOCK, LANES), lambda i, p, t: (i, t, p)),
        out_shape=jax.ShapeDtypeStruct((b, L, D_ATTN), F32),
        compiler_params=_cparams(("arbitrary", "arbitrary", "arbitrary")),
        name="moba_prompt",
    )(q3, k3, v3, km, ft)
    return o.reshape(b * L, D_ATTN)


def _moba_sample_kernel(pt_ref, q_ref, kn_ref, vn_ref, kc_ref, vc_ref, eb_ref, o_ref,
                        qst_ref, s_ref, p_ref, oacc_ref, *, t, npages, past):
    ph = pl.program_id(1)
    pg = pl.program_id(2)
    r = ATTN_HEADS * t
    nfull = past // MOBA_BLOCK
    lane_d = lax.broadcasted_iota(jnp.int32, (r, D_ATTN), 1)
    row_d = lax.broadcasted_iota(jnp.int32, (r, D_ATTN), 0)
    own_lanes = lane_d // ATTN_HEAD_DIM == row_d // t

    @pl.when((ph == 0) & (pg == 0))
    def _():
        qst_ref[...] = jnp.where(own_lanes, jnp.concatenate([q_ref[0]] * ATTN_HEADS, axis=0), 0.0)

    @pl.when(ph == 0)
    def _():
        col = pl.multiple_of(pg * PAGE_SIZE, PAGE_SIZE)
        s_ref[:, pl.ds(col, PAGE_SIZE)] = _dot(qst_ref[...].astype(BF16), kc_ref[0, 0].astype(BF16))

    @pl.when((ph == 0) & (pg == npages - 1))
    def _():
        qst = qst_ref[...]
        lane = lax.broadcasted_iota(jnp.int32, (r, LANES), 1)
        gate = jnp.zeros((r, LANES), F32)
        for n in range(nfull):
            g_n = jnp.sum(s_ref[:, n * MOBA_BLOCK:(n + 1) * MOBA_BLOCK], -1, keepdims=True) * (1.0 / MOBA_BLOCK)
            gate = jnp.where(lane == n, g_n, gate)
        sel = _select_topk(gate, lane, nfull)
        selb = jnp.where(sel, 0.0, NEG_BIG).astype(BF16)
        slope = _alibi_slopes_col(t, r)
        t_row = lax.broadcasted_iota(jnp.int32, (r, 1), 0) % t
        key_pos = lax.broadcasted_iota(jnp.int32, (r, past), 1)
        dist = (past + t_row - key_pos).astype(F32)
        s = s_ref[...] - slope * dist + _dot(selb, eb_ref[...])
        own_i = lax.broadcasted_iota(jnp.int32, (r, t), 1)
        s_own = _dot_nt(qst, kn_ref[0], precision=HI) - slope * (t_row - own_i).astype(F32)
        s_own = jnp.where(own_i <= t_row, s_own, NEG_BIG)
        m = jnp.maximum(jnp.max(s, -1, keepdims=True), jnp.max(s_own, -1, keepdims=True))
        pe = jnp.exp(s - m)
        pe_own = jnp.exp(s_own - m)
        inv = 1.0 / (jnp.sum(pe, -1, keepdims=True) + jnp.sum(pe_own, -1, keepdims=True))
        p_ref[...] = (pe * inv).astype(BF16)
        oacc_ref[...] = jnp.dot(pe_own * inv, vn_ref[0], precision=HI, preferred_element_type=F32)

    @pl.when(ph == 1)
    def _():
        col = pl.multiple_of(pg * PAGE_SIZE, PAGE_SIZE)
        oacc_ref[...] += _dot_nt(p_ref[:, pl.ds(col, PAGE_SIZE)], vc_ref[0, 0].astype(BF16))

    @pl.when((ph == 1) & (pg == npages - 1))
    def _():
        oa = jnp.where(own_lanes, oacc_ref[...], 0.0)
        o = oa[0:t]
        for h in range(1, ATTN_HEADS):
            o = o + oa[h * t:(h + 1) * t]
        o_ref[0] = o


def moba_sample_attn(q, kn, vn, cache_k, cache_v, page_table, layer):
    b, t, _ = q.shape
    npages = page_table.shape[1]
    past = npages * PAGE_SIZE
    assert past % MOBA_BLOCK == 0 and past // MOBA_BLOCK <= LANES
    r = ATTN_HEADS * t
    kc = jnp.transpose(cache_k, (0, 1, 3, 4, 2)).reshape(cache_k.shape[0], cache_k.shape[1], D_ATTN, PAGE_SIZE)
    vc = jnp.transpose(cache_v, (0, 1, 3, 4, 2)).reshape(cache_v.shape[0], cache_v.shape[1], D_ATTN, PAGE_SIZE)
    eb = (jnp.arange(LANES, dtype=jnp.int32)[:, None] == jnp.arange(past, dtype=jnp.int32)[None, :] // MOBA_BLOCK)
    eb = eb.astype(BF16)
    last = npages - 1
    grid_spec = pltpu.PrefetchScalarGridSpec(
        num_scalar_prefetch=1,
        grid=(b, 2, npages),
        in_specs=[pl.BlockSpec((1, t, D_ATTN), lambda s, ph, pg, pt: (s, 0, 0)),
                  pl.BlockSpec((1, t, D_ATTN), lambda s, ph, pg, pt: (s, 0, 0)),
                  pl.BlockSpec((1, t, D_ATTN), lambda s, ph, pg, pt: (s, 0, 0)),
                  pl.BlockSpec((1, 1, D_ATTN, PAGE_SIZE),
                               lambda s, ph, pg, pt: (layer, pt[s, jnp.where(ph == 0, pg, last)], 0, 0)),
                  pl.BlockSpec((1, 1, D_ATTN, PAGE_SIZE),
                               lambda s, ph, pg, pt: (layer, pt[s, jnp.where(ph == 0, 0, pg)], 0, 0)),
                  pl.BlockSpec((LANES, past), lambda s, ph, pg, pt: (0, 0))],
        out_specs=pl.BlockSpec((1, t, D_ATTN), lambda s, ph, pg, pt: (s, 0, 0)),
        scratch_shapes=[pltpu.VMEM((r, D_ATTN), F32), pltpu.VMEM((r, past), F32), pltpu.VMEM((r, past), BF16),
                        pltpu.VMEM((r, D_ATTN), F32)],
    )
    return pl.pallas_call(
        functools.partial(_moba_sample_kernel, t=t, npages=npages, past=past),
        grid_spec=grid_spec,
        out_shape=jax.ShapeDtypeStruct((b, t, D_ATTN), F32),
        compiler_params=_cparams(("arbitrary", "arbitrary", "arbitrary")),
        name="moba_sample",
    )(page_table, q, kn, vn, kc, vc, eb)


def _row(v, width=None):
    v = v.reshape(1, -1).astype(F32)
    if width is not None and v.shape[1] < width:
        v = jnp.pad(v, ((0, 0), (0, width - v.shape[1])))
    return v


def _even_weights(w_in, conv_w, conv_b, dt_bias, a_log, d_skip, norm_g, dw_w, dw_b, ln_g, ln_b, w_out):
    s1 = D_SSM
    s2 = s1 + SSM_CONV_DIM
    s3 = s2 + SSM_HEADS
    w_dt = jnp.pad(w_in[:, s2:s3], ((0, 0), (0, LANES - SSM_HEADS)))
    w_perm = jnp.concatenate([w_in[:, :s2], w_in[:, s3:], w_dt], axis=1).astype(BF16)
    return dict(
        w_in=w_perm, conv_w=conv_w, conv_b=_row(conv_b), dt_bias=_row(dt_bias, LANES), a_log=_row(a_log, LANES),
        d_skip=_row(jnp.repeat(d_skip, SSM_HEAD_DIM)), norm_g=_row(norm_g), dw_w=dw_w, dw_b=_row(dw_b),
        ln_g=_row(ln_g), ln_b=_row(ln_b), w_out_y=w_out[:D_SSM].astype(BF16), w_out_u=w_out[D_SSM:].astype(BF16))


def _even_mixer(x, b, L, conv_buf, ssm_state, conf_buf, ew, mix_g, mix_b):
    zx, u, dtr = proj_even(x, ew["w_in"])
    y, new_state, new_conv = ssd_mixer(zx.reshape(b, L, ZX_W), dtr.reshape(b, L, LANES), conv_buf, ssm_state,
                                       ew["conv_w"], ew["conv_b"], ew["dt_bias"], ew["a_log"], ew["d_skip"],
                                       ew["norm_g"])
    uc, new_conf = conf_mixer(u.reshape(b, L, D_CONF), conf_buf, ew["dw_w"], ew["dw_b"], ew["ln_g"], ew["ln_b"])
    x = out_proj_ln(x, [y.reshape(b * L, D_SSM), uc.reshape(b * L, D_CONF)], [ew["w_out_y"], ew["w_out_u"]],
                    mix_g, mix_b)
    return x, new_state, new_conv, new_conf


def kernel(x_prompt, x_sample, p_prompt, p_sample, state_ssm, state_ssm_conv, state_conf_conv, cache_k, cache_v, page_table, w_in_even, ssm_conv_w, ssm_conv_b, ssm_dt_bias, ssm_a_log, ssm_d, ssm_norm_g, conf_dw_w, conf_dw_b, conf_ln_g, conf_ln_b, w_out_even, w_qkv, w_o, ln_mix_g, ln_mix_b, ln_ffn_g, ln_ffn_b, w_ffn_gate, w_ffn_up, w_ffn_down, w_router, w_exp_gate, w_exp_up, w_exp_down, w_ple, w_ple_gate):
    bp, lp, _ = x_prompt.shape
    bs, ls, _ = x_sample.shape
    xp = x_prompt.reshape(bp * lp, D_MODEL)
    xs = x_sample.reshape(bs * ls, D_MODEL)
    outs = {k: [] for k in ("ssm_p", "sconv_p", "conf_p", "k_p", "v_p", "ssm_s", "sconv_s", "conf_s", "k_s", "v_s")}
    no_router = jnp.zeros((D_MODEL, LANES), F32)
    for i in range(DEPTH):
        li = i // 2
        mix_g, mix_b = _row(ln_mix_g[i]), _row(ln_mix_b[i])
        ffn_g, ffn_b = _row(ln_ffn_g[i]), _row(ln_ffn_b[i])
        if i % 2 == 0:
            ew = _even_weights(w_in_even[li], ssm_conv_w[li], ssm_conv_b[li], ssm_dt_bias[li], ssm_a_log[li],
                               ssm_d[li], ssm_norm_g[li], conf_dw_w[li], conf_dw_b[li], conf_ln_g[li],
                               conf_ln_b[li], w_out_even[li])
            zc = jnp.zeros((bp, SSM_CONV - 1, SSM_CONV_DIM), F32)
            zs = jnp.zeros((bp, SSM_HEADS, SSM_HEAD_DIM, SSM_STATE), F32)
            zf = jnp.zeros((bp, CONF_WIDTH - 1, D_CONF), F32)
            xp, st_p, cv_p, cf_p = _even_mixer(xp, bp, lp, zc, zs, zf, ew, mix_g, mix_b)
            xs, st_s, cv_s, cf_s = _even_mixer(xs, bs, ls, state_ssm_conv[li], state_ssm[li], state_conf_conv[li],
                                               ew, mix_g, mix_b)
            outs["ssm_p"].append(st_p); outs["sconv_p"].append(cv_p); outs["conf_p"].append(cf_p)
            outs["ssm_s"].append(st_s); outs["sconv_s"].append(cv_s); outs["conf_s"].append(cf_s)
            wg = w_ffn_gate[li][None].astype(BF16)
            wu = w_ffn_up[li][None].astype(BF16)
            wd = w_ffn_down[li][None].astype(BF16)
            xp = ffn_ln(xp, no_router, wg, wu, wd, ffn_g, ffn_b, routed=False)
            xs = ffn_ln(xs, no_router, wg, wu, wd, ffn_g, ffn_b, routed=False)
        else:
            wqkv = w_qkv[li].astype(BF16)
            wo = w_o[li].astype(BF16)
            qp, kp, vp, kp16, vp16, kmean = qkv_proj(xp, wqkv)
            qs, ks, vs, _, _, _ = qkv_proj(xs, wqkv)
            hp = moba_prompt_attn(qp, kp16, vp16, kmean.reshape(bp, lp // MOBA_BLOCK, D_ATTN), bp, lp)
            hs = moba_sample_attn(qs.reshape(bs, ls, D_ATTN), ks.reshape(bs, ls, D_ATTN), vs.reshape(bs, ls, D_ATTN),
                                  cache_k, cache_v, page_table, li)
            shp_p = (bp, lp, ATTN_HEADS, ATTN_HEAD_DIM)
            shp_s = (bs, ls, ATTN_HEADS, ATTN_HEAD_DIM)
            outs["k_p"].append(kp.reshape(shp_p)); outs["v_p"].append(vp.reshape(shp_p))
            outs["k_s"].append(ks.reshape(shp_s)); outs["v_s"].append(vs.reshape(shp_s))
            xp = out_proj_ln(xp, [hp], [wo], mix_g, mix_b)
            xs = out_proj_ln(xs, [hs.reshape(bs * ls, D_ATTN)], [wo], mix_g, mix_b)
            wr = jnp.pad(w_router[li], ((0, 0), (0, LANES - N_EXPERTS)))
            wg = w_exp_gate[li].astype(BF16)
            wu = w_exp_up[li].astype(BF16)
            wd = w_exp_down[li].astype(BF16)
            xp = ffn_ln(xp, wr, wg, wu, wd, ffn_g, ffn_b, routed=True)
            xs = ffn_ln(xs, wr, wg, wu, wd, ffn_g, ffn_b, routed=True)
        wpg = w_ple_gate[i].astype(BF16)
        wpp = w_ple[i].astype(BF16)
        xp = ple(xp, p_prompt[i].reshape(bp * lp, PLE_DIM), wpg, wpp)
        xs = ple(xs, p_sample[i].reshape(bs * ls, PLE_DIM), wpg, wpp)
    st = lambda k: jnp.stack(outs[k])
    return (xp.reshape(bp, lp, D_MODEL), xs.reshape(bs, ls, D_MODEL),
            st("ssm_p"), st("sconv_p"), st("conf_p"), st("k_p"), st("v_p"),
            st("ssm_s"), st("sconv_s"), st("conf_s"), st("k_s"), st("v_s"))
```

```python
import functools
import math

import jax
import jax.numpy as jnp
from jax import lax
from jax.experimental import pallas as pl
from jax.experimental.pallas import tpu as pltpu

F32 = jnp.float32
BF16 = jnp.bfloat16
HI = lax.Precision.HIGHEST

D_MODEL = 1024
DEPTH = 4
PAGE_SIZE = 128
SSM_HEADS = 16
SSM_HEAD_DIM = 64
D_SSM = SSM_HEADS * SSM_HEAD_DIM
SSM_GROUPS = 2
SSM_STATE = 128
SSM_CONV = 4
SSD_CHUNK = 128
SSM_CONV_DIM = D_SSM + 2 * SSM_GROUPS * SSM_STATE
D_CONF = 1024
CONF_WIDTH = 31
ATTN_HEADS = 16
ATTN_HEAD_DIM = 64
D_ATTN = ATTN_HEADS * ATTN_HEAD_DIM
MOBA_BLOCK = 256
MOBA_TOPK = 3
Q_BLOCK = 128
D_FF = 2816
N_EXPERTS = 8
EXPERT_TOPK = 2
PLE_DIM = 256
DEEPNORM_ALPHA = (2.0 * DEPTH) ** 0.25
LN_EPS = 1e-5

LANES = 128
NEG_BIG = -1e30
LOG2E = math.log2(math.e)
VMEM_LIMIT = 56 * 1024 * 1024
N_PAIRS = ATTN_HEADS // 2
FEAT_SEL0 = 4
SAMPLE_PAGES_PER_STEP = 8


def _cparams(sem):
    return pltpu.CompilerParams(dimension_semantics=sem, vmem_limit_bytes=VMEM_LIMIT)


def _silu(x):
    return x * jax.nn.sigmoid(x)


def _layer_norm(x, g, b):
    xc = x - jnp.mean(x, -1, keepdims=True)
    var = jnp.mean(xc * xc, -1, keepdims=True)
    return xc * lax.rsqrt(var + LN_EPS) * g + b


def _dot(a, b):
    return jnp.dot(a, b, preferred_element_type=F32)


def _dot_nt(a, b, precision=None):
    return lax.dot_general(a, b, (((1,), (1,)), ((), ())), precision=precision,
                           preferred_element_type=F32)


def _dot_tn(a, b):
    return lax.dot_general(a, b, (((0,), (0,)), ((), ())), preferred_element_type=F32)


ZX_W = D_SSM + SSM_CONV_DIM


def _proj_even_kernel(x_ref, w_ref, zx_ref, u_ref, dt_ref):
    x = x_ref[...].astype(BF16)
    zx_ref[...] = _dot(x, w_ref[:, 0:ZX_W])
    a = _dot(x, w_ref[:, ZX_W:ZX_W + D_CONF])
    g = _dot(x, w_ref[:, ZX_W + D_CONF:ZX_W + 2 * D_CONF])
    u_ref[...] = a * jax.nn.sigmoid(g)
    dt_ref[...] = _dot(x, w_ref[:, ZX_W + 2 * D_CONF:])


def proj_even(x, w, tm=256):
    m = x.shape[0]
    tm = min(tm, m)
    assert m % tm == 0
    n = w.shape[1]
    return pl.pallas_call(
        _proj_even_kernel,
        grid=(m // tm,),
        in_specs=[pl.BlockSpec((tm, D_MODEL), lambda i: (i, 0)),
                  pl.BlockSpec((D_MODEL, n), lambda i: (0, 0))],
        out_specs=[pl.BlockSpec((tm, ZX_W), lambda i: (i, 0)),
                   pl.BlockSpec((tm, D_CONF), lambda i: (i, 0)),
                   pl.BlockSpec((tm, LANES), lambda i: (i, 0))],
        out_shape=[jax.ShapeDtypeStruct((m, ZX_W), F32),
                   jax.ShapeDtypeStruct((m, D_CONF), F32),
                   jax.ShapeDtypeStruct((m, LANES), F32)],
        compiler_params=_cparams(("arbitrary",)),
        name="proj_even",
    )(x, w)


SSM_PAD = 8
SSM_OFF = SSM_PAD - (SSM_CONV - 1)


def _ssd_kernel(zx_ref, dtr_ref, cbuf_ref, s0_ref, cw_ref, cb_ref, dtb_ref, alog_ref, dsk_ref, ng_ref,
                y_ref, sout_ref, cout_ref, ext_ref, st_ref, ybuf_ref, *, q, nc):
    c = pl.program_id(1)

    @pl.when(c == 0)
    def _():
        ext_ref[0:SSM_PAD, :] = jnp.zeros((SSM_PAD, SSM_CONV_DIM), F32)
        ext_ref[SSM_OFF:SSM_PAD, :] = cbuf_ref[0]
        st_ref[...] = s0_ref[0]

    ext_ref[SSM_PAD:SSM_PAD + q, :] = zx_ref[0, :, D_SSM:ZX_W]
    acc = jnp.broadcast_to(cb_ref[...], (q, SSM_CONV_DIM))
    for k in range(SSM_CONV):
        acc = acc + ext_ref[SSM_OFF + k:SSM_OFF + k + q, :] * cw_ref[k:k + 1, :]
    xbc = _silu(acc)

    @pl.when(c == nc - 1)
    def _():
        cout_ref[0] = ext_ref[q + SSM_OFF:q + SSM_PAD, :]

    ext_ref[0:SSM_PAD, :] = ext_ref[q:q + SSM_PAD, :]

    bm = xbc[:, D_SSM:D_SSM + SSM_GROUPS * SSM_STATE]
    cm = xbc[:, D_SSM + SSM_GROUPS * SSM_STATE:]
    xdt = dtr_ref[0] + dtb_ref[...]
    dt = jnp.maximum(xdt, 0.0) + jnp.log(1.0 + jnp.exp(-jnp.abs(xdt)))
    a = -jnp.exp(alog_ref[...])
    dta = dt * a
    rows = lax.broadcasted_iota(jnp.int32, (q, q), 0)
    cols = lax.broadcasted_iota(jnp.int32, (q, q), 1)
    tril = rows >= cols
    acum = jnp.dot(tril.astype(F32), dta, precision=HI, preferred_element_type=F32)
    acum_t = acum.T
    dt_t = dt.T
    wend = jnp.exp(acum[q - 1:q, :] - acum) * dt
    ea = jnp.exp(acum)
    dec_t = jnp.exp(acum_t[:, q - 1:q])

    bmb = bm.astype(BF16)
    cmb = cm.astype(BF16)
    cb = [_dot_nt(cmb[:, g * SSM_STATE:(g + 1) * SSM_STATE], bmb[:, g * SSM_STATE:(g + 1) * SSM_STATE])
          for g in range(SSM_GROUPS)]
    lane = lax.broadcasted_iota(jnp.int32, (q, LANES), 1)
    is_a = lane < SSM_HEAD_DIM
    sub = lax.broadcasted_iota(jnp.int32, (LANES, LANES), 0)
    hpg = SSM_HEADS // SSM_GROUPS

    def scores(h, g):
        seg = acum[:, h:h + 1] - acum_t[h:h + 1, :]
        dec = jnp.where(tril, jnp.exp(jnp.where(tril, seg, 0.0)), 0.0)
        return (cb[g] * dec * dt_t[h:h + 1, :]).astype(BF16)

    for p in range(SSM_HEADS // 2):
        ha, hb = 2 * p, 2 * p + 1
        g = ha // hpg
        xp = xbc[:, p * LANES:(p + 1) * LANES]
        xa = jnp.where(is_a, xp, 0.0).astype(BF16)
        xb = jnp.where(is_a, 0.0, xp).astype(BF16)
        y = _dot(scores(ha, g), xa) + _dot(scores(hb, g), xb)
        sp = st_ref[p]
        ea_p = jnp.where(is_a, ea[:, ha:ha + 1], ea[:, hb:hb + 1])
        y = y + ea_p * _dot_nt(cmb[:, g * SSM_STATE:(g + 1) * SSM_STATE], sp.astype(BF16))
        y = y + dsk_ref[:, p * LANES:(p + 1) * LANES] * xp
        ybuf_ref[:, p * LANES:(p + 1) * LANES] = y
        wend_p = jnp.where(is_a, wend[:, ha:ha + 1], wend[:, hb:hb + 1])
        upd = _dot_tn((xp * wend_p).astype(BF16), bmb[:, g * SSM_STATE:(g + 1) * SSM_STATE])
        dec_p = jnp.where(sub < SSM_HEAD_DIM, dec_t[ha:ha + 1, :], dec_t[hb:hb + 1, :])
        st_ref[p] = dec_p * sp + upd

    z = zx_ref[0, :, 0:D_SSM]
    yy = ybuf_ref[...] * _silu(z)
    y_ref[0] = yy * lax.rsqrt(jnp.mean(yy * yy, -1, keepdims=True) + LN_EPS) * ng_ref[...]

    @pl.when(c == nc - 1)
    def _():
        sout_ref[0] = st_ref[...]


def ssd_mixer(zx, dtr, cbuf, s0, cw, cb, dtb, alog, dsk, ng):
    b, L, _ = zx.shape
    q = SSD_CHUNK if L % SSD_CHUNK == 0 else L
    nc = L // q
    npair = SSM_HEADS // 2
    s0p = s0.reshape(b, npair, 2 * SSM_HEAD_DIM, SSM_STATE)
    const = lambda shape: pl.BlockSpec(shape, lambda i, j: (0,) * len(shape))
    y, sout, cout = pl.pallas_call(
        functools.partial(_ssd_kernel, q=q, nc=nc),
        grid=(b, nc),
        in_specs=[pl.BlockSpec((1, q, ZX_W), lambda i, j: (i, j, 0)),
                  pl.BlockSpec((1, q, LANES), lambda i, j: (i, j, 0)),
                  pl.BlockSpec((1, SSM_CONV - 1, SSM_CONV_DIM), lambda i, j: (i, 0, 0)),
                  pl.BlockSpec((1, npair, 2 * SSM_HEAD_DIM, SSM_STATE), lambda i, j: (i, 0, 0, 0)),
                  const((SSM_CONV, SSM_CONV_DIM)), const((1, SSM_CONV_DIM)),
                  const((1, LANES)), const((1, LANES)), const((1, D_SSM)), const((1, D_SSM))],
        out_specs=[pl.BlockSpec((1, q, D_SSM), lambda i, j: (i, j, 0)),
                   pl.BlockSpec((1, npair, 2 * SSM_HEAD_DIM, SSM_STATE), lambda i, j: (i, 0, 0, 0)),
                   pl.BlockSpec((1, SSM_CONV - 1, SSM_CONV_DIM), lambda i, j: (i, 0, 0))],
        out_shape=[jax.ShapeDtypeStruct((b, L, D_SSM), F32),
                   jax.ShapeDtypeStruct((b, npair, 2 * SSM_HEAD_DIM, SSM_STATE), F32),
                   jax.ShapeDtypeStruct((b, SSM_CONV - 1, SSM_CONV_DIM), F32)],
        scratch_shapes=[pltpu.VMEM((q + SSM_PAD, SSM_CONV_DIM), F32),
                        pltpu.VMEM((npair, 2 * SSM_HEAD_DIM, SSM_STATE), F32),
                        pltpu.VMEM((q, D_SSM), F32)],
        compiler_params=_cparams(("arbitrary", "arbitrary")),
        name="ssd_mixer",
    )(zx, dtr, cbuf, s0p, cw, cb, dtb, alog, dsk, ng)
    return y, sout.reshape(b, SSM_HEADS, SSM_HEAD_DIM, SSM_STATE), cout


CONF_PAD = 32
CONF_OFF = CONF_PAD - (CONF_WIDTH - 1)
CONF_ROWS = 32


def _conf_kernel(u_ref, buf_ref, w_ref, b_ref, g_ref, beta_ref, o_ref, bout_ref, ext_ref, acc_ref, *, t, nt):
    i = pl.program_id(1)

    @pl.when(i == 0)
    def _():
        ext_ref[0:CONF_PAD, :] = jnp.zeros((CONF_PAD, D_CONF), F32)
        ext_ref[CONF_OFF:CONF_PAD, :] = buf_ref[0]

    ext_ref[CONF_PAD:CONF_PAD + t, :] = u_ref[0]
    rr = min(CONF_ROWS, t)

    def chan_body(ci, carry):
        c0 = pl.multiple_of(ci * LANES, LANES)
        for r0 in range(0, t, rr):
            acc = jnp.broadcast_to(b_ref[:, pl.ds(c0, LANES)], (rr, LANES))
            for k in range(CONF_WIDTH):
                acc = acc + ext_ref[r0 + CONF_OFF + k:r0 + CONF_OFF + k + rr, pl.ds(c0, LANES)] * w_ref[k:k + 1, pl.ds(c0, LANES)]
            acc_ref[r0:r0 + rr, pl.ds(c0, LANES)] = acc
        return carry

    lax.fori_loop(0, D_CONF // LANES, chan_body, 0)
    o_ref[0] = _silu(_layer_norm(acc_ref[...], g_ref[...], beta_ref[...]))

    @pl.when(i == nt - 1)
    def _():
        bout_ref[0] = ext_ref[t + CONF_OFF:t + CONF_PAD, :]

    if nt > 1:
        ext_ref[0:CONF_PAD, :] = ext_ref[t:t + CONF_PAD, :]


def conf_mixer(u, buf, w, bias, g, beta, t_max=128):
    b, L, _ = u.shape
    t = t_max if L % t_max == 0 else L
    nt = L // t
    assert nt == 1 or t >= CONF_PAD
    const = lambda shape: pl.BlockSpec(shape, lambda i, j: (0,) * len(shape))
    return pl.pallas_call(
        functools.partial(_conf_kernel, t=t, nt=nt),
        grid=(b, nt),
        in_specs=[pl.BlockSpec((1, t, D_CONF), lambda i, j: (i, j, 0)),
                  pl.BlockSpec((1, CONF_WIDTH - 1, D_CONF), lambda i, j: (i, 0, 0)),
                  const((CONF_WIDTH, D_CONF)), const((1, D_CONF)), const((1, D_CONF)), const((1, D_CONF))],
        out_specs=[pl.BlockSpec((1, t, D_CONF), lambda i, j: (i, j, 0)),
                   pl.BlockSpec((1, CONF_WIDTH - 1, D_CONF), lambda i, j: (i, 0, 0))],
        out_shape=[jax.ShapeDtypeStruct((b, L, D_CONF), F32),
                   jax.ShapeDtypeStruct((b, CONF_WIDTH - 1, D_CONF), F32)],
        scratch_shapes=[pltpu.VMEM((t + CONF_PAD, D_CONF), F32), pltpu.VMEM((t, D_CONF), F32)],
        compiler_params=_cparams(("arbitrary", "arbitrary")),
        name="conf_mixer",
    )(u, buf, w, bias, g, beta)


def _out_ln_kernel(*refs, n_in):
    x_ref = refs[0]
    h_refs = refs[1:1 + n_in]
    w_refs = refs[1 + n_in:1 + 2 * n_in]
    g_ref, b_ref, o_ref = refs[1 + 2 * n_in:]
    acc = DEEPNORM_ALPHA * x_ref[...]
    for h_ref, w_ref in zip(h_refs, w_refs):
        acc = acc + _dot(h_ref[...].astype(BF16), w_ref[...])
    o_ref[...] = _layer_norm(acc, g_ref[...], b_ref[...])


def out_proj_ln(x, hs, ws, g, b, tm=512):
    m = x.shape[0]
    tm = min(tm, m)
    assert m % tm == 0
    n_in = len(hs)
    row = lambda k: pl.BlockSpec((tm, k), lambda i: (i, 0))
    const = lambda shape: pl.BlockSpec(shape, lambda i: (0,) * len(shape))
    return pl.pallas_call(
        functools.partial(_out_ln_kernel, n_in=n_in),
        grid=(m // tm,),
        in_specs=[row(D_MODEL)] + [row(h.shape[1]) for h in hs] + [const(w.shape) for w in ws]
                 + [const((1, D_MODEL)), const((1, D_MODEL))],
        out_specs=row(D_MODEL),
        out_shape=jax.ShapeDtypeStruct((m, D_MODEL), F32),
        compiler_params=_cparams(("arbitrary",)),
        name="out_proj_ln",
    )(x, *hs, *ws, g, b)


def _ffn_kernel(x_ref, wg_ref, wu_ref, wd_ref, g_ref, b_ref, o_ref, acc_ref, xb_ref, *, nf):
    f = pl.program_id(1)

    @pl.when(f == 0)
    def _():
        acc_ref[...] = jnp.zeros_like(acc_ref)
        xb_ref[...] = x_ref[...].astype(BF16)

    xb = xb_ref[...]
    h = _silu(_dot(xb, wg_ref[...])) * _dot(xb, wu_ref[...])
    acc_ref[...] += _dot(h.astype(BF16), wd_ref[...])

    @pl.when(f == nf - 1)
    def _():
        o_ref[...] = _layer_norm(DEEPNORM_ALPHA * x_ref[...] + acc_ref[...], g_ref[...], b_ref[...])


def ffn_ln(x, wg, wu, wd, g, b, tm=512, tf=1408):
    m = x.shape[0]
    tm = min(tm, m)
    assert m % tm == 0 and D_FF % tf == 0
    nf = D_FF // tf
    const = lambda shape: pl.BlockSpec(shape, lambda i, f: (0,) * len(shape))
    return pl.pallas_call(
        functools.partial(_ffn_kernel, nf=nf),
        grid=(m // tm, nf),
        in_specs=[pl.BlockSpec((tm, D_MODEL), lambda i, f: (i, 0)),
                  pl.BlockSpec((D_MODEL, tf), lambda i, f: (0, f)),
                  pl.BlockSpec((D_MODEL, tf), lambda i, f: (0, f)),
                  pl.BlockSpec((tf, D_MODEL), lambda i, f: (f, 0)),
                  const((1, D_MODEL)), const((1, D_MODEL))],
        out_specs=pl.BlockSpec((tm, D_MODEL), lambda i, f: (i, 0)),
        out_shape=jax.ShapeDtypeStruct((m, D_MODEL), F32),
        scratch_shapes=[pltpu.VMEM((tm, D_MODEL), F32), pltpu.VMEM((tm, D_MODEL), BF16)],
        compiler_params=_cparams(("arbitrary", "arbitrary")),
        name="ffn_ln",
    )(x, wg, wu, wd, g, b)


MOE_CHUNK = 128


def _moe_kernel(x_ref, wr_ref, wg_ref, wu_ref, wd_ref, g_ref, b_ref, o_ref,
                acc_ref, gates_t_ref, rank_ref, rank_t_ref, xb_ref, xs_ref, y_ref, wrow_ref, cnt_ref,
                *, ne, nf, tm):
    e = pl.program_id(1)
    f = pl.program_id(2)
    nch = tm // MOE_CHUNK

    @pl.when((e == 0) & (f == 0))
    def _route():
        x = x_ref[...]
        acc_ref[...] = jnp.zeros_like(acc_ref)
        xb_ref[...] = x.astype(BF16)
        lane = lax.broadcasted_iota(jnp.int32, (tm, LANES), 1)
        logits = jnp.dot(x, wr_ref[...], precision=HI, preferred_element_type=F32)
        logits = jnp.where(lane < ne, logits, -jnp.inf)
        v1 = jnp.max(logits, -1, keepdims=True)
        i1 = jnp.min(jnp.where(logits == v1, lane, LANES), -1, keepdims=True)
        rest = jnp.where(lane == i1, -jnp.inf, logits)
        v2 = jnp.max(rest, -1, keepdims=True)
        i2 = jnp.min(jnp.where(rest == v2, lane, LANES), -1, keepdims=True)
        e2 = jnp.exp(v2 - v1)
        w1 = 1.0 / (1.0 + e2)
        gates = jnp.where(lane == i1, w1, 0.0) + jnp.where(lane == i2, e2 * w1, 0.0)
        routed = (lane == i1) | (lane == i2)
        cm = jnp.where(routed, 1.0, 0.0)
        cmb = cm.astype(BF16)
        col = lax.broadcasted_iota(jnp.int32, (MOE_CHUNK, tm), 1)
        row = lax.broadcasted_iota(jnp.int32, (MOE_CHUNK, tm), 0)
        for rc in range(nch):
            earlier = jnp.where(col < row + rc * MOE_CHUNK, 1.0, 0.0).astype(BF16)
            rows = slice(rc * MOE_CHUNK, (rc + 1) * MOE_CHUNK)
            rank_ref[rows, :] = jnp.where(routed[rows], _dot(earlier, cmb), -1.0)
        rank_t_ref[...] = rank_ref[...].T
        gates_t_ref[...] = gates.T
        for ee in range(ne):
            cnt_ref[ee] = jnp.sum(jnp.where(lane == ee, cm, 0.0)).astype(jnp.int32)

    cnt = cnt_ref[e]

    @pl.when(f == 0)
    def _gather():
        r_row = rank_t_ref[pl.ds(e, 1), :]
        g_row = gates_t_ref[pl.ds(e, 1), :]
        slot = lax.broadcasted_iota(jnp.int32, (MOE_CHUNK, tm), 0)
        for c in range(nch):
            @pl.when(c * MOE_CHUNK < cnt)
            def _():
                rows = slice(c * MOE_CHUNK, (c + 1) * MOE_CHUNK)
                hit = r_row == (slot + c * MOE_CHUNK).astype(F32)
                xs_ref[rows, :] = _dot(jnp.where(hit, 1.0, 0.0).astype(BF16), xb_ref[...]).astype(BF16)
                w = jnp.sum(jnp.where(hit, g_row, 0.0), -1, keepdims=True)
                wrow_ref[rows, :] = jnp.broadcast_to(w, (MOE_CHUNK, LANES))

    for c in range(nch):
        @pl.when(c * MOE_CHUNK < cnt)
        def _():
            rows = slice(c * MOE_CHUNK, (c + 1) * MOE_CHUNK)
            xc = xs_ref[rows, :]
            h = _silu(_dot(xc, wg_ref[0])) * _dot(xc, wu_ref[0])
            yv = _dot(h.astype(BF16), wd_ref[0])

            @pl.when(f == 0)
            def _():
                y_ref[rows, :] = yv

            @pl.when(f > 0)
            def _():
                y_ref[rows, :] += yv

    @pl.when(f == nf - 1)
    def _combine():
        lane = lax.broadcasted_iota(jnp.int32, (tm, LANES), 1)
        r_col = jnp.sum(jnp.where(lane == e, rank_ref[...], 0.0), -1, keepdims=True)
        slot = lax.broadcasted_iota(jnp.int32, (tm, MOE_CHUNK), 1)
        for c in range(nch):
            @pl.when(c * MOE_CHUNK < cnt)
            def _():
                rows = slice(c * MOE_CHUNK, (c + 1) * MOE_CHUNK)
                hit_t = jnp.where(r_col == (slot + c * MOE_CHUNK).astype(F32), 1.0, 0.0).astype(BF16)
                ys = (y_ref[rows, :] * wrow_ref[rows, 0:1]).astype(BF16)
                acc_ref[...] += _dot(hit_t, ys)

    @pl.when((e == ne - 1) & (f == nf - 1))
    def _():
        o_ref[...] = _layer_norm(DEEPNORM_ALPHA * x_ref[...] + acc_ref[...], g_ref[...], b_ref[...])


def moe_ln(x, w_router, wg, wu, wd, g, b, tm=1024, tf=1408):
    m = x.shape[0]
    tm = min(tm, m)
    assert m % tm == 0 and D_FF % tf == 0 and tm % MOE_CHUNK == 0
    ne = wg.shape[0]
    nf = D_FF // tf
    const = lambda shape: pl.BlockSpec(shape, lambda i, e, f: (0,) * len(shape))
    return pl.pallas_call(
        functools.partial(_moe_kernel, ne=ne, nf=nf, tm=tm),
        grid=(m // tm, ne, nf),
        in_specs=[pl.BlockSpec((tm, D_MODEL), lambda i, e, f: (i, 0)),
                  const((D_MODEL, LANES)),
                  pl.BlockSpec((1, D_MODEL, tf), lambda i, e, f: (e, 0, f)),
                  pl.BlockSpec((1, D_MODEL, tf), lambda i, e, f: (e, 0, f)),
                  pl.BlockSpec((1, tf, D_MODEL), lambda i, e, f: (e, f, 0)),
                  const((1, D_MODEL)), const((1, D_MODEL))],
        out_specs=pl.BlockSpec((tm, D_MODEL), lambda i, e, f: (i, 0)),
        out_shape=jax.ShapeDtypeStruct((m, D_MODEL), F32),
        scratch_shapes=[pltpu.VMEM((tm, D_MODEL), F32),
                        pltpu.VMEM((LANES, tm), F32),
                        pltpu.VMEM((tm, LANES), F32),
                        pltpu.VMEM((LANES, tm), F32),
                        pltpu.VMEM((tm, D_MODEL), BF16),
                        pltpu.VMEM((tm, D_MODEL), BF16),
                        pltpu.VMEM((tm, D_MODEL), F32),
                        pltpu.VMEM((tm, LANES), F32),
                        pltpu.SMEM((ne,), jnp.int32)],
        compiler_params=_cparams(("arbitrary", "arbitrary", "arbitrary")),
        name="moe_ln",
    )(x, w_router, wg, wu, wd, g, b)


def _ple_kernel(x_ref, p_ref, wg_ref, wp_ref, o_ref):
    x = x_ref[...]
    gate = jax.nn.sigmoid(_dot(x.astype(BF16), wg_ref[...]))
    o_ref[...] = x + gate * _dot(p_ref[...].astype(BF16), wp_ref[...])


def ple(x, p, wg, wp, tm=512):
    m = x.shape[0]
    tm = min(tm, m)
    assert m % tm == 0
    return pl.pallas_call(
        _ple_kernel,
        grid=(m // tm,),
        in_specs=[pl.BlockSpec((tm, D_MODEL), lambda i: (i, 0)),
                  pl.BlockSpec((tm, PLE_DIM), lambda i: (i, 0)),
                  pl.BlockSpec((D_MODEL, D_MODEL), lambda i: (0, 0)),
                  pl.BlockSpec((PLE_DIM, D_MODEL), lambda i: (0, 0))],
        out_specs=pl.BlockSpec((tm, D_MODEL), lambda i: (i, 0)),
        out_shape=jax.ShapeDtypeStruct((m, D_MODEL), F32),
        compiler_params=_cparams(("arbitrary",)),
        name="ple",
    )(x, p, wg, wp)


def _qkv_kernel(x_ref, w_ref, q_ref, k_ref, v_ref, k16_ref, v16_ref, km_ref, *, tm):
    x = x_ref[...].astype(BF16)
    q_ref[...] = _dot(x, w_ref[:, 0:D_ATTN]) * (ATTN_HEAD_DIM ** -0.5)
    k = _dot(x, w_ref[:, D_ATTN:2 * D_ATTN])
    v = _dot(x, w_ref[:, 2 * D_ATTN:])
    k_ref[...] = k
    v_ref[...] = v
    k16_ref[...] = k.astype(BF16)
    v16_ref[...] = v.astype(BF16)
    for j in range(tm // MOBA_BLOCK):
        km_ref[0, j:j + 1, :] = jnp.mean(k[j * MOBA_BLOCK:(j + 1) * MOBA_BLOCK], 0, keepdims=True)


def qkv_proj(x, w, tm=512):
    m = x.shape[0]
    tm = min(tm, m)
    assert m % tm == 0
    assert tm % MOBA_BLOCK == 0
    nb = tm // MOBA_BLOCK
    row = lambda: pl.BlockSpec((tm, D_ATTN), lambda i: (i, 0))
    return pl.pallas_call(
        functools.partial(_qkv_kernel, tm=tm),
        grid=(m // tm,),
        in_specs=[pl.BlockSpec((tm, D_MODEL), lambda i: (i, 0)),
                  pl.BlockSpec((D_MODEL, 3 * D_ATTN), lambda i: (0, 0))],
        out_specs=[row(), row(), row(), row(), row(), pl.BlockSpec((1, nb, D_ATTN), lambda i: (i, 0, 0))],
        out_shape=[jax.ShapeDtypeStruct((m, D_ATTN), F32)] * 3 + [jax.ShapeDtypeStruct((m, D_ATTN), BF16)] * 2
                  + [jax.ShapeDtypeStruct((m // tm, nb, D_ATTN), F32)],
        compiler_params=_cparams(("arbitrary",)),
        name="qkv_proj",
    )(x, w)


def _select_topk(gate, nidx, n_cand):
    g = jnp.where((nidx >= 0) & (nidx < n_cand), gate, -jnp.inf)
    sel = jnp.zeros(gate.shape, jnp.bool_)
    big = jnp.int32(1 << 20)
    for _ in range(MOBA_TOPK):
        m = jnp.max(g, -1, keepdims=True)
        hit = (g == m) & (m > -jnp.inf)
        first = jnp.min(jnp.where(hit, nidx, big), -1, keepdims=True)
        pick = hit & (nidx == first)
        sel = sel | pick
        g = jnp.where(pick, -jnp.inf, g)
    return sel


def _alibi_slopes_col(rows_per_head, n_rows):
    h = lax.broadcasted_iota(jnp.int32, (n_rows, 1), 0) // rows_per_head
    return jnp.exp2(-8.0 * (h + 1).astype(F32) / ATTN_HEADS)


def _moba_prompt_kernel(q_ref, k_ref, v_ref, km_ref, ft_ref, o_ref, *, npp):
    pp = pl.program_id(1)
    j = pl.program_id(2)
    tq = MOBA_BLOCK
    lane = lax.broadcasted_iota(jnp.int32, (tq, LANES), 1)
    is_a = lane < ATTN_HEAD_DIM
    lane2 = lax.broadcasted_iota(jnp.int32, (2 * tq, LANES), 1)
    row2 = lax.broadcasted_iota(jnp.int32, (2 * tq, LANES), 0)
    nidx = lane2 - FEAT_SEL0
    t_row = lax.broadcasted_iota(jnp.int32, (2 * tq, MOBA_BLOCK), 0) % tq
    key_i = lax.broadcasted_iota(jnp.int32, (2 * tq, MOBA_BLOCK), 1)
    causal = key_i <= t_row

    def block_scores(u, qf, n):
        start = pl.multiple_of(n * MOBA_BLOCK, MOBA_BLOCK)
        cols = slice(u * LANES, (u + 1) * LANES)
        kf = jnp.concatenate([k_ref[0, pl.ds(start, MOBA_BLOCK), cols], ft_ref[n]], axis=1)
        return _dot_nt(qf, kf), v_ref[0, pl.ds(start, MOBA_BLOCK), cols]

    qfs, carry0 = [], []
    for u in range(npp):
        q = q_ref[0, :, u * LANES:(u + 1) * LANES]
        q2 = jnp.concatenate([jnp.where(is_a, q, 0.0), jnp.where(is_a, 0.0, q)], axis=0)
        gate = _dot_nt(q2, km_ref[0, :, u * LANES:(u + 1) * LANES], precision=HI)
        sel = _select_topk(gate, nidx, j)
        head = 2 * (pp * npp + u) + row2 // tq
        slope = jnp.exp2(-8.0 * (head + 1).astype(F32) / ATTN_HEADS) * LOG2E
        s_hi = slope.astype(BF16).astype(F32)
        s_lo = slope - s_hi
        feat = jnp.where(sel | (nidx == j), 0.0, NEG_BIG)
        feat = jnp.where(lane2 < 2, s_hi, jnp.where(lane2 < FEAT_SEL0, s_lo, feat))
        qf = jnp.concatenate([q2 * LOG2E, feat], axis=1).astype(BF16)
        s, vb = block_scores(u, qf, j)
        s = jnp.where(causal, s, NEG_BIG)
        m0 = jnp.max(s, -1, keepdims=True)
        p0 = jnp.exp2(s - m0)
        qfs.append(qf)
        carry0.append((m0, jnp.sum(p0, -1, keepdims=True), _dot(p0.astype(BF16), vb)))

    def body(n, carry):
        out = []
        for u in range(npp):
            m, l, acc = carry[u]
            s, vb = block_scores(u, qfs[u], n)
            m_new = jnp.maximum(m, jnp.max(s, -1, keepdims=True))
            alpha = jnp.exp2(m - m_new)
            pn = jnp.exp2(s - m_new)
            out.append((m_new, alpha * l + jnp.sum(pn, -1, keepdims=True),
                        alpha * acc + _dot(pn.astype(BF16), vb)))
        return tuple(out)

    final = lax.fori_loop(0, j, body, tuple(carry0))
    for u in range(npp):
        _, l, acc = final[u]
        o = acc / l
        o_ref[0, :, u * LANES:(u + 1) * LANES] = jnp.where(is_a, o[0:tq], o[tq:])


def moba_prompt_attn(q, k16, v16, kmean, b, L, npp=2):
    assert L % MOBA_BLOCK == 0 and MOBA_BLOCK % Q_BLOCK == 0 and N_PAIRS % npp == 0
    nblk = L // MOBA_BLOCK
    assert FEAT_SEL0 + nblk <= LANES
    q3 = q.reshape(b, L, D_ATTN)
    k3 = k16.reshape(b, L, D_ATTN)
    v3 = v16.reshape(b, L, D_ATTN)
    km = jnp.pad(kmean, ((0, 0), (FEAT_SEL0, LANES - FEAT_SEL0 - nblk), (0, 0)))
    n_i = jnp.arange(nblk, dtype=jnp.int32)[:, None, None]
    key_i = jnp.arange(MOBA_BLOCK, dtype=jnp.int32)[None, :, None]
    ln = jnp.arange(LANES, dtype=jnp.int32)[None, None, :]
    ft = jnp.where((ln == 0) | (ln == 2), n_i * MOBA_BLOCK,
                   jnp.where((ln == 1) | (ln == 3), key_i, (ln == n_i + FEAT_SEL0).astype(jnp.int32)))
    ft = ft.astype(BF16)
    o = pl.pallas_call(
        functools.partial(_moba_prompt_kernel, npp=npp),
        grid=(b, N_PAIRS // npp, nblk),
        in_specs=[pl.BlockSpec((1, MOBA_BLOCK, npp * LANES), lambda i, p, t: (i, t, p)),
                  pl.BlockSpec((1, L, npp * LANES), lambda i, p, t: (i, 0, p)),
                  pl.BlockSpec((1, L, npp * LANES), lambda i, p, t: (i, 0, p)),
                  pl.BlockSpec((1, LANES, npp * LANES), lambda i, p, t: (i, 0, p)),
                  pl.BlockSpec((nblk, MOBA_BLOCK, LANES), lambda i, p, t: (0, 0, 0))],
        out_specs=pl.BlockSpec((1, MOBA_BLOCK, npp * LANES), lambda i, p, t: (i, t, p)),
        out_shape=jax.ShapeDtypeStruct((b, L, D_ATTN), F32),
        compiler_params=_cparams(("arbitrary", "arbitrary", "arbitrary")),
        name="moba_prompt",
    )(q3, k3, v3, km, ft)
    return o.reshape(b * L, D_ATTN)


def _moba_sample_kernel(pt_ref, q_ref, kn_ref, vn_ref, *refs, t, ngroups, gp, past):
    kc_refs = refs[0:gp]
    vc_refs = refs[gp:2 * gp]
    eb_ref, o_ref, qst_ref, qstb_ref, s_ref, p_ref, oacc_ref = refs[2 * gp:]
    ph = pl.program_id(1)
    grp = pl.program_id(2)
    r = ATTN_HEADS * t
    nfull = past // MOBA_BLOCK
    lane_d = lax.broadcasted_iota(jnp.int32, (r, D_ATTN), 1)
    row_d = lax.broadcasted_iota(jnp.int32, (r, D_ATTN), 0)
    own_lanes = lane_d // ATTN_HEAD_DIM == row_d // t

    @pl.when((ph == 0) & (grp == 0))
    def _():
        qst = jnp.where(own_lanes, jnp.concatenate([q_ref[0]] * ATTN_HEADS, axis=0), 0.0)
        qst_ref[...] = qst
        qstb_ref[...] = qst.astype(BF16)

    @pl.when(ph == 0)
    def _():
        for g in range(gp):
            col = pl.multiple_of((grp * gp + g) * PAGE_SIZE, PAGE_SIZE)
            s_ref[:, pl.ds(col, PAGE_SIZE)] = _dot(qstb_ref[...], kc_refs[g][0, 0].astype(BF16))

    @pl.when((ph == 0) & (grp == ngroups - 1))
    def _():
        qst = qst_ref[...]
        lane = lax.broadcasted_iota(jnp.int32, (r, LANES), 1)
        gate = jnp.zeros((r, LANES), F32)
        for n in range(nfull):
            g_n = jnp.sum(s_ref[:, n * MOBA_BLOCK:(n + 1) * MOBA_BLOCK], -1, keepdims=True) * (1.0 / MOBA_BLOCK)
            gate = jnp.where(lane == n, g_n, gate)
        sel = _select_topk(gate, lane, nfull)
        selb = jnp.where(sel, 0.0, NEG_BIG).astype(BF16)
        slope = _alibi_slopes_col(t, r)
        t_row = lax.broadcasted_iota(jnp.int32, (r, 1), 0) % t
        key_pos = lax.broadcasted_iota(jnp.int32, (r, past), 1)
        dist = (past + t_row - key_pos).astype(F32)
        s = s_ref[...] - slope * dist + _dot(selb, eb_ref[...])
        own_i = lax.broadcasted_iota(jnp.int32, (r, t), 1)
        s_own = _dot_nt(qst, kn_ref[0], precision=HI) - slope * (t_row - own_i).astype(F32)
        s_own = jnp.where(own_i <= t_row, s_own, NEG_BIG)
        m = jnp.maximum(jnp.max(s, -1, keepdims=True), jnp.max(s_own, -1, keepdims=True))
        pe = jnp.exp(s - m)
        pe_own = jnp.exp(s_own - m)
        inv = 1.0 / (jnp.sum(pe, -1, keepdims=True) + jnp.sum(pe_own, -1, keepdims=True))
        p_ref[...] = (pe * inv).astype(BF16)
        oacc_ref[...] = jnp.dot(pe_own * inv, vn_ref[0], precision=HI, preferred_element_type=F32)

    @pl.when(ph == 1)
    def _():
        acc = oacc_ref[...]
        for g in range(gp):
            col = pl.multiple_of((grp * gp + g) * PAGE_SIZE, PAGE_SIZE)
            acc = acc + _dot_nt(p_ref[:, pl.ds(col, PAGE_SIZE)], vc_refs[g][0, 0].astype(BF16))
        oacc_ref[...] = acc

    @pl.when((ph == 1) & (grp == ngroups - 1))
    def _():
        oa = jnp.where(own_lanes, oacc_ref[...], 0.0)
        o = oa[0:t]
        for h in range(1, ATTN_HEADS):
            o = o + oa[h * t:(h + 1) * t]
        o_ref[0] = o


def moba_sample_attn(q, kn, vn, cache_k, cache_v, page_table, layer):
    b, t, _ = q.shape
    npages = page_table.shape[1]
    past = npages * PAGE_SIZE
    assert past % MOBA_BLOCK == 0 and past // MOBA_BLOCK <= LANES
    r = ATTN_HEADS * t
    kc = jnp.transpose(cache_k, (0, 1, 3, 4, 2)).reshape(cache_k.shape[0], cache_k.shape[1], D_ATTN, PAGE_SIZE)
    vc = jnp.transpose(cache_v, (0, 1, 3, 4, 2)).reshape(cache_v.shape[0], cache_v.shape[1], D_ATTN, PAGE_SIZE)
    eb = (jnp.arange(LANES, dtype=jnp.int32)[:, None] == jnp.arange(past, dtype=jnp.int32)[None, :] // MOBA_BLOCK)
    eb = eb.astype(BF16)
    gp = SAMPLE_PAGES_PER_STEP if npages % SAMPLE_PAGES_PER_STEP == 0 else 1
    ngroups = npages // gp
    tok = pl.BlockSpec((1, t, D_ATTN), lambda s, ph, grp, pt: (s, 0, 0))

    def k_spec(g):
        return pl.BlockSpec((1, 1, D_ATTN, PAGE_SIZE), lambda s, ph, grp, pt:
                            (layer, pt[s, jnp.where(ph == 0, grp, ngroups - 1) * gp + g], 0, 0))

    def v_spec(g):
        return pl.BlockSpec((1, 1, D_ATTN, PAGE_SIZE), lambda s, ph, grp, pt:
                            (layer, pt[s, jnp.where(ph == 0, 0, grp) * gp + g], 0, 0))

    grid_spec = pltpu.PrefetchScalarGridSpec(
        num_scalar_prefetch=1,
        grid=(b, 2, ngroups),
        in_specs=[tok, tok, tok] + [k_spec(g) for g in range(gp)] + [v_spec(g) for g in range(gp)]
                 + [pl.BlockSpec((LANES, past), lambda s, ph, grp, pt: (0, 0))],
        out_specs=tok,
        scratch_shapes=[pltpu.VMEM((r, D_ATTN), F32), pltpu.VMEM((r, D_ATTN), BF16), pltpu.VMEM((r, past), F32),
                        pltpu.VMEM((r, past), BF16), pltpu.VMEM((r, D_ATTN), F32)],
    )
    return pl.pallas_call(
        functools.partial(_moba_sample_kernel, t=t, ngroups=ngroups, gp=gp, past=past),
        grid_spec=grid_spec,
        out_shape=jax.ShapeDtypeStruct((b, t, D_ATTN), F32),
        compiler_params=_cparams(("arbitrary", "arbitrary", "arbitrary")),
        name="moba_sample",
    )(page_table, q, kn, vn, *([kc] * gp), *([vc] * gp), eb)


def _row(v, width=None):
    v = v.reshape(1, -1).astype(F32)
    if width is not None and v.shape[1] < width:
        v = jnp.pad(v, ((0, 0), (0, width - v.shape[1])))
    return v


def _even_weights(w_in, conv_w, conv_b, dt_bias, a_log, d_skip, norm_g, dw_w, dw_b, ln_g, ln_b, w_out):
    s1 = D_SSM
    s2 = s1 + SSM_CONV_DIM
    s3 = s2 + SSM_HEADS
    w_dt = jnp.pad(w_in[:, s2:s3], ((0, 0), (0, LANES - SSM_HEADS)))
    w_perm = jnp.concatenate([w_in[:, :s2], w_in[:, s3:], w_dt], axis=1).astype(BF16)
    return dict(
        w_in=w_perm, conv_w=conv_w, conv_b=_row(conv_b), dt_bias=_row(dt_bias, LANES), a_log=_row(a_log, LANES),
        d_skip=_row(jnp.repeat(d_skip, SSM_HEAD_DIM)), norm_g=_row(norm_g), dw_w=dw_w, dw_b=_row(dw_b),
        ln_g=_row(ln_g), ln_b=_row(ln_b), w_out_y=w_out[:D_SSM].astype(BF16), w_out_u=w_out[D_SSM:].astype(BF16))


def _even_mixer(x, b, L, conv_buf, ssm_state, conf_buf, ew, mix_g, mix_b):
    zx, u, dtr = proj_even(x, ew["w_in"])
    y, new_state, new_conv = ssd_mixer(zx.reshape(b, L, ZX_W), dtr.reshape(b, L, LANES), conv_buf, ssm_state,
                                       ew["conv_w"], ew["conv_b"], ew["dt_bias"], ew["a_log"], ew["d_skip"],
                                       ew["norm_g"])
    uc, new_conf = conf_mixer(u.reshape(b, L, D_CONF), conf_buf, ew["dw_w"], ew["dw_b"], ew["ln_g"], ew["ln_b"])
    x = out_proj_ln(x, [y.reshape(b * L, D_SSM), uc.reshape(b * L, D_CONF)], [ew["w_out_y"], ew["w_out_u"]],
                    mix_g, mix_b)
    return x, new_state, new_conv, new_conf


def kernel(x_prompt, x_sample, p_prompt, p_sample, state_ssm, state_ssm_conv, state_conf_conv, cache_k, cache_v, page_table, w_in_even, ssm_conv_w, ssm_conv_b, ssm_dt_bias, ssm_a_log, ssm_d, ssm_norm_g, conf_dw_w, conf_dw_b, conf_ln_g, conf_ln_b, w_out_even, w_qkv, w_o, ln_mix_g, ln_mix_b, ln_ffn_g, ln_ffn_b, w_ffn_gate, w_ffn_up, w_ffn_down, w_router, w_exp_gate, w_exp_up, w_exp_down, w_ple, w_ple_gate):
    bp, lp, _ = x_prompt.shape
    bs, ls, _ = x_sample.shape
    xp = x_prompt.reshape(bp * lp, D_MODEL)
    xs = x_sample.reshape(bs * ls, D_MODEL)
    outs = {k: [] for k in ("ssm_p", "sconv_p", "conf_p", "k_p", "v_p", "ssm_s", "sconv_s", "conf_s", "k_s", "v_s")}
    for i in range(DEPTH):
        li = i // 2
        mix_g, mix_b = _row(ln_mix_g[i]), _row(ln_mix_b[i])
        ffn_g, ffn_b = _row(ln_ffn_g[i]), _row(ln_ffn_b[i])
        if i % 2 == 0:
            ew = _even_weights(w_in_even[li], ssm_conv_w[li], ssm_conv_b[li], ssm_dt_bias[li], ssm_a_log[li],
                               ssm_d[li], ssm_norm_g[li], conf_dw_w[li], conf_dw_b[li], conf_ln_g[li],
                               conf_ln_b[li], w_out_even[li])
            zc = jnp.zeros((bp, SSM_CONV - 1, SSM_CONV_DIM), F32)
            zs = jnp.zeros((bp, SSM_HEADS, SSM_HEAD_DIM, SSM_STATE), F32)
            zf = jnp.zeros((bp, CONF_WIDTH - 1, D_CONF), F32)
            xp, st_p, cv_p, cf_p = _even_mixer(xp, bp, lp, zc, zs, zf, ew, mix_g, mix_b)
            xs, st_s, cv_s, cf_s = _even_mixer(xs, bs, ls, state_ssm_conv[li], state_ssm[li], state_conf_conv[li],
                                               ew, mix_g, mix_b)
            outs["ssm_p"].append(st_p); outs["sconv_p"].append(cv_p); outs["conf_p"].append(cf_p)
            outs["ssm_s"].append(st_s); outs["sconv_s"].append(cv_s); outs["conf_s"].append(cf_s)
            wg = w_ffn_gate[li].astype(BF16)
            wu = w_ffn_up[li].astype(BF16)
            wd = w_ffn_down[li].astype(BF16)
            xp = ffn_ln(xp, wg, wu, wd, ffn_g, ffn_b)
            xs = ffn_ln(xs, wg, wu, wd, ffn_g, ffn_b)
        else:
            wqkv = w_qkv[li].astype(BF16)
            wo = w_o[li].astype(BF16)
            qp, kp, vp, kp16, vp16, kmean = qkv_proj(xp, wqkv)
            qs, ks, vs, _, _, _ = qkv_proj(xs, wqkv)
            hp = moba_prompt_attn(qp, kp16, vp16, kmean.reshape(bp, lp // MOBA_BLOCK, D_ATTN), bp, lp)
            hs = moba_sample_attn(qs.reshape(bs, ls, D_ATTN), ks.reshape(bs, ls, D_ATTN), vs.reshape(bs, ls, D_ATTN),
                                  cache_k, cache_v, page_table, li)
            shp_p = (bp, lp, ATTN_HEADS, ATTN_HEAD_DIM)
            shp_s = (bs, ls, ATTN_HEADS, ATTN_HEAD_DIM)
            outs["k_p"].append(kp.reshape(shp_p)); outs["v_p"].append(vp.reshape(shp_p))
            outs["k_s"].append(ks.reshape(shp_s)); outs["v_s"].append(vs.reshape(shp_s))
            xp = out_proj_ln(xp, [hp], [wo], mix_g, mix_b)
            xs = out_proj_ln(xs, [hs.reshape(bs * ls, D_ATTN)], [wo], mix_g, mix_b)
            wr = jnp.pad(w_router[li], ((0, 0), (0, LANES - N_EXPERTS)))
            wg = w_exp_gate[li].astype(BF16)
            wu = w_exp_up[li].astype(BF16)
            wd = w_exp_down[li].astype(BF16)
            xp = moe_ln(xp, wr, wg, wu, wd, ffn_g, ffn_b)
            xs = moe_ln(xs, wr, wg, wu, wd, ffn_g, ffn_b)
        wpg = w_ple_gate[i].astype(BF16)
        wpp = w_ple[i].astype(BF16)
        xp = ple(xp, p_prompt[i].reshape(bp * lp, PLE_DIM), wpg, wpp)
        xs = ple(xs, p_sample[i].reshape(bs * ls, PLE_DIM), wpg, wpp)
    st = lambda k: jnp.stack(outs[k])
    return (xp.reshape(bp, lp, D_MODEL), xs.reshape(bs, ls, D_MODEL),
            st("ssm_p"), st("sconv_p"), st("conf_p"), st("k_p"), st("v_p"),
            st("ssm_s"), st("sconv_s"), st("conf_s"), st("k_s"), st("v_s"))
```

```python
import functools
import math

import jax
import jax.numpy as jnp
from jax import lax
from jax.experimental import pallas as pl
from jax.experimental.pallas import tpu as pltpu

F32 = jnp.float32
BF16 = jnp.bfloat16
HI = lax.Precision.HIGHEST

D_MODEL = 1024
DEPTH = 4
PAGE_SIZE = 128
SSM_HEADS = 16
SSM_HEAD_DIM = 64
D_SSM = SSM_HEADS * SSM_HEAD_DIM
SSM_GROUPS = 2
SSM_STATE = 128
SSM_CONV = 4
SSD_CHUNK = 128
SSM_CONV_DIM = D_SSM + 2 * SSM_GROUPS * SSM_STATE
D_CONF = 1024
CONF_WIDTH = 31
ATTN_HEADS = 16
ATTN_HEAD_DIM = 64
D_ATTN = ATTN_HEADS * ATTN_HEAD_DIM
MOBA_BLOCK = 256
MOBA_TOPK = 3
Q_BLOCK = 128
D_FF = 2816
N_EXPERTS = 8
EXPERT_TOPK = 2
PLE_DIM = 256
DEEPNORM_ALPHA = (2.0 * DEPTH) ** 0.25
LN_EPS = 1e-5

LANES = 128
NEG_BIG = -1e30
LOG2E = math.log2(math.e)
VMEM_LIMIT = 56 * 1024 * 1024
N_PAIRS = ATTN_HEADS // 2
FEAT_SEL0 = 4
SAMPLE_PAGES_PER_STEP = 8


def _cparams(sem):
    return pltpu.CompilerParams(dimension_semantics=sem, vmem_limit_bytes=VMEM_LIMIT)


def _silu(x):
    return x * jax.nn.sigmoid(x)


def _layer_norm(x, g, b):
    xc = x - jnp.mean(x, -1, keepdims=True)
    var = jnp.mean(xc * xc, -1, keepdims=True)
    return xc * lax.rsqrt(var + LN_EPS) * g + b


def _dot(a, b):
    return jnp.dot(a, b, preferred_element_type=F32)


def _dot_nt(a, b, precision=None):
    return lax.dot_general(a, b, (((1,), (1,)), ((), ())), precision=precision,
                           preferred_element_type=F32)


def _dot_tn(a, b):
    return lax.dot_general(a, b, (((0,), (0,)), ((), ())), preferred_element_type=F32)


ZX_W = D_SSM + SSM_CONV_DIM


def _proj_even_kernel(x_ref, w_ref, zx_ref, u_ref, dt_ref):
    x = x_ref[...].astype(BF16)
    zx_ref[...] = _dot(x, w_ref[:, 0:ZX_W])
    a = _dot(x, w_ref[:, ZX_W:ZX_W + D_CONF])
    g = _dot(x, w_ref[:, ZX_W + D_CONF:ZX_W + 2 * D_CONF])
    u_ref[...] = a * jax.nn.sigmoid(g)
    dt_ref[...] = _dot(x, w_ref[:, ZX_W + 2 * D_CONF:])


def proj_even(x, w, tm=256):
    m = x.shape[0]
    tm = min(tm, m)
    assert m % tm == 0
    n = w.shape[1]
    return pl.pallas_call(
        _proj_even_kernel,
        grid=(m // tm,),
        in_specs=[pl.BlockSpec((tm, D_MODEL), lambda i: (i, 0)),
                  pl.BlockSpec((D_MODEL, n), lambda i: (0, 0))],
        out_specs=[pl.BlockSpec((tm, ZX_W), lambda i: (i, 0)),
                   pl.BlockSpec((tm, D_CONF), lambda i: (i, 0)),
                   pl.BlockSpec((tm, LANES), lambda i: (i, 0))],
        out_shape=[jax.ShapeDtypeStruct((m, ZX_W), F32),
                   jax.ShapeDtypeStruct((m, D_CONF), F32),
                   jax.ShapeDtypeStruct((m, LANES), F32)],
        compiler_params=_cparams(("arbitrary",)),
        name="proj_even",
    )(x, w)


SSM_PAD = 8
SSM_OFF = SSM_PAD - (SSM_CONV - 1)


def _ssd_kernel(zx_ref, dtr_ref, cbuf_ref, s0_ref, cw_ref, cb_ref, dtb_ref, alog_ref, dsk_ref, ng_ref,
                y_ref, sout_ref, cout_ref, ext_ref, st_ref, ybuf_ref, *, q, nc):
    c = pl.program_id(1)

    @pl.when(c == 0)
    def _():
        ext_ref[0:SSM_PAD, :] = jnp.zeros((SSM_PAD, SSM_CONV_DIM), F32)
        ext_ref[SSM_OFF:SSM_PAD, :] = cbuf_ref[0]
        st_ref[...] = s0_ref[0]

    ext_ref[SSM_PAD:SSM_PAD + q, :] = zx_ref[0, :, D_SSM:ZX_W]
    acc = jnp.broadcast_to(cb_ref[...], (q, SSM_CONV_DIM))
    for k in range(SSM_CONV):
        acc = acc + ext_ref[SSM_OFF + k:SSM_OFF + k + q, :] * cw_ref[k:k + 1, :]
    xbc = _silu(acc)

    @pl.when(c == nc - 1)
    def _():
        cout_ref[0] = ext_ref[q + SSM_OFF:q + SSM_PAD, :]

    ext_ref[0:SSM_PAD, :] = ext_ref[q:q + SSM_PAD, :]

    bm = xbc[:, D_SSM:D_SSM + SSM_GROUPS * SSM_STATE]
    cm = xbc[:, D_SSM + SSM_GROUPS * SSM_STATE:]
    xdt = dtr_ref[0] + dtb_ref[...]
    dt = jnp.maximum(xdt, 0.0) + jnp.log(1.0 + jnp.exp(-jnp.abs(xdt)))
    a = -jnp.exp(alog_ref[...])
    dta = dt * a
    rows = lax.broadcasted_iota(jnp.int32, (q, q), 0)
    cols = lax.broadcasted_iota(jnp.int32, (q, q), 1)
    tril = rows >= cols
    acum = jnp.dot(tril.astype(F32), dta, precision=HI, preferred_element_type=F32)
    acum_t = acum.T
    dt_t = dt.T
    wend = jnp.exp(acum[q - 1:q, :] - acum) * dt
    ea = jnp.exp(acum)
    dec_t = jnp.exp(acum_t[:, q - 1:q])

    bmb = bm.astype(BF16)
    cmb = cm.astype(BF16)
    cb = [_dot_nt(cmb[:, g * SSM_STATE:(g + 1) * SSM_STATE], bmb[:, g * SSM_STATE:(g + 1) * SSM_STATE])
          for g in range(SSM_GROUPS)]
    lane = lax.broadcasted_iota(jnp.int32, (q, LANES), 1)
    is_a = lane < SSM_HEAD_DIM
    sub = lax.broadcasted_iota(jnp.int32, (LANES, LANES), 0)
    hpg = SSM_HEADS // SSM_GROUPS

    def scores(h, g):
        seg = acum[:, h:h + 1] - acum_t[h:h + 1, :]
        dec = jnp.where(tril, jnp.exp(jnp.where(tril, seg, 0.0)), 0.0)
        return (cb[g] * dec * dt_t[h:h + 1, :]).astype(BF16)

    for p in range(SSM_HEADS // 2):
        ha, hb = 2 * p, 2 * p + 1
        g = ha // hpg
        xp = xbc[:, p * LANES:(p + 1) * LANES]
        xa = jnp.where(is_a, xp, 0.0).astype(BF16)
        xb = jnp.where(is_a, 0.0, xp).astype(BF16)
        y = _dot(scores(ha, g), xa) + _dot(scores(hb, g), xb)
        sp = st_ref[p]
        ea_p = jnp.where(is_a, ea[:, ha:ha + 1], ea[:, hb:hb + 1])
        y = y + ea_p * _dot_nt(cmb[:, g * SSM_STATE:(g + 1) * SSM_STATE], sp.astype(BF16))
        y = y + dsk_ref[:, p * LANES:(p + 1) * LANES] * xp
        ybuf_ref[:, p * LANES:(p + 1) * LANES] = y
        wend_p = jnp.where(is_a, wend[:, ha:ha + 1], wend[:, hb:hb + 1])
        upd = _dot_tn((xp * wend_p).astype(BF16), bmb[:, g * SSM_STATE:(g + 1) * SSM_STATE])
        dec_p = jnp.where(sub < SSM_HEAD_DIM, dec_t[ha:ha + 1, :], dec_t[hb:hb + 1, :])
        st_ref[p] = dec_p * sp + upd

    z = zx_ref[0, :, 0:D_SSM]
    yy = ybuf_ref[...] * _silu(z)
    y_ref[0] = yy * lax.rsqrt(jnp.mean(yy * yy, -1, keepdims=True) + LN_EPS) * ng_ref[...]

    @pl.when(c == nc - 1)
    def _():
        sout_ref[0] = st_ref[...]


def ssd_mixer(zx, dtr, cbuf, s0, cw, cb, dtb, alog, dsk, ng):
    b, L, _ = zx.shape
    q = SSD_CHUNK if L % SSD_CHUNK == 0 else L
    nc = L // q
    npair = SSM_HEADS // 2
    s0p = s0.reshape(b, npair, 2 * SSM_HEAD_DIM, SSM_STATE)
    const = lambda shape: pl.BlockSpec(shape, lambda i, j: (0,) * len(shape))
    y, sout, cout = pl.pallas_call(
        functools.partial(_ssd_kernel, q=q, nc=nc),
        grid=(b, nc),
        in_specs=[pl.BlockSpec((1, q, ZX_W), lambda i, j: (i, j, 0)),
                  pl.BlockSpec((1, q, LANES), lambda i, j: (i, j, 0)),
                  pl.BlockSpec((1, SSM_CONV - 1, SSM_CONV_DIM), lambda i, j: (i, 0, 0)),
                  pl.BlockSpec((1, npair, 2 * SSM_HEAD_DIM, SSM_STATE), lambda i, j: (i, 0, 0, 0)),
                  const((SSM_CONV, SSM_CONV_DIM)), const((1, SSM_CONV_DIM)),
                  const((1, LANES)), const((1, LANES)), const((1, D_SSM)), const((1, D_SSM))],
        out_specs=[pl.BlockSpec((1, q, D_SSM), lambda i, j: (i, j, 0)),
                   pl.BlockSpec((1, npair, 2 * SSM_HEAD_DIM, SSM_STATE), lambda i, j: (i, 0, 0, 0)),
                   pl.BlockSpec((1, SSM_CONV - 1, SSM_CONV_DIM), lambda i, j: (i, 0, 0))],
        out_shape=[jax.ShapeDtypeStruct((b, L, D_SSM), F32),
                   jax.ShapeDtypeStruct((b, npair, 2 * SSM_HEAD_DIM, SSM_STATE), F32),
                   jax.ShapeDtypeStruct((b, SSM_CONV - 1, SSM_CONV_DIM), F32)],
        scratch_shapes=[pltpu.VMEM((q + SSM_PAD, SSM_CONV_DIM), F32),
                        pltpu.VMEM((npair, 2 * SSM_HEAD_DIM, SSM_STATE), F32),
                        pltpu.VMEM((q, D_SSM), F32)],
        compiler_params=_cparams(("arbitrary", "arbitrary")),
        name="ssd_mixer",
    )(zx, dtr, cbuf, s0p, cw, cb, dtb, alog, dsk, ng)
    return y, sout.reshape(b, SSM_HEADS, SSM_HEAD_DIM, SSM_STATE), cout


CONF_PAD = 32
CONF_OFF = CONF_PAD - (CONF_WIDTH - 1)
CONF_ROWS = 32


def _conf_kernel(u_ref, buf_ref, w_ref, b_ref, g_ref, beta_ref, o_ref, bout_ref, ext_ref, acc_ref, *, t, nt):
    i = pl.program_id(1)

    @pl.when(i == 0)
    def _():
        ext_ref[0:CONF_PAD, :] = jnp.zeros((CONF_PAD, D_CONF), F32)
        ext_ref[CONF_OFF:CONF_PAD, :] = buf_ref[0]

    ext_ref[CONF_PAD:CONF_PAD + t, :] = u_ref[0]
    rr = min(CONF_ROWS, t)

    def chan_body(ci, carry):
        c0 = pl.multiple_of(ci * LANES, LANES)
        for r0 in range(0, t, rr):
            acc = jnp.broadcast_to(b_ref[:, pl.ds(c0, LANES)], (rr, LANES))
            for k in range(CONF_WIDTH):
                acc = acc + ext_ref[r0 + CONF_OFF + k:r0 + CONF_OFF + k + rr, pl.ds(c0, LANES)] * w_ref[k:k + 1, pl.ds(c0, LANES)]
            acc_ref[r0:r0 + rr, pl.ds(c0, LANES)] = acc
        return carry

    lax.fori_loop(0, D_CONF // LANES, chan_body, 0)
    o_ref[0] = _silu(_layer_norm(acc_ref[...], g_ref[...], beta_ref[...]))

    @pl.when(i == nt - 1)
    def _():
        bout_ref[0] = ext_ref[t + CONF_OFF:t + CONF_PAD, :]

    if nt > 1:
        ext_ref[0:CONF_PAD, :] = ext_ref[t:t + CONF_PAD, :]


def conf_mixer(u, buf, w, bias, g, beta, t_max=128):
    b, L, _ = u.shape
    t = t_max if L % t_max == 0 else L
    nt = L // t
    assert nt == 1 or t >= CONF_PAD
    const = lambda shape: pl.BlockSpec(shape, lambda i, j: (0,) * len(shape))
    return pl.pallas_call(
        functools.partial(_conf_kernel, t=t, nt=nt),
        grid=(b, nt),
        in_specs=[pl.BlockSpec((1, t, D_CONF), lambda i, j: (i, j, 0)),
                  pl.BlockSpec((1, CONF_WIDTH - 1, D_CONF), lambda i, j: (i, 0, 0)),
                  const((CONF_WIDTH, D_CONF)), const((1, D_CONF)), const((1, D_CONF)), const((1, D_CONF))],
        out_specs=[pl.BlockSpec((1, t, D_CONF), lambda i, j: (i, j, 0)),
                   pl.BlockSpec((1, CONF_WIDTH - 1, D_CONF), lambda i, j: (i, 0, 0))],
        out_shape=[jax.ShapeDtypeStruct((b, L, D_CONF), F32),
                   jax.ShapeDtypeStruct((b, CONF_WIDTH - 1, D_CONF), F32)],
        scratch_shapes=[pltpu.VMEM((t + CONF_PAD, D_CONF), F32), pltpu.VMEM((t, D_CONF), F32)],
        compiler_params=_cparams(("arbitrary", "arbitrary")),
        name="conf_mixer",
    )(u, buf, w, bias, g, beta)


def _out_ln_kernel(*refs, n_in):
    x_ref = refs[0]
    h_refs = refs[1:1 + n_in]
    w_refs = refs[1 + n_in:1 + 2 * n_in]
    g_ref, b_ref, o_ref = refs[1 + 2 * n_in:]
    acc = DEEPNORM_ALPHA * x_ref[...]
    for h_ref, w_ref in zip(h_refs, w_refs):
        acc = acc + _dot(h_ref[...].astype(BF16), w_ref[...])
    o_ref[...] = _layer_norm(acc, g_ref[...], b_ref[...])


def out_proj_ln(x, hs, ws, g, b, tm=512):
    m = x.shape[0]
    tm = min(tm, m)
    assert m % tm == 0
    n_in = len(hs)
    row = lambda k: pl.BlockSpec((tm, k), lambda i: (i, 0))
    const = lambda shape: pl.BlockSpec(shape, lambda i: (0,) * len(shape))
    return pl.pallas_call(
        functools.partial(_out_ln_kernel, n_in=n_in),
        grid=(m // tm,),
        in_specs=[row(D_MODEL)] + [row(h.shape[1]) for h in hs] + [const(w.shape) for w in ws]
                 + [const((1, D_MODEL)), const((1, D_MODEL))],
        out_specs=row(D_MODEL),
        out_shape=jax.ShapeDtypeStruct((m, D_MODEL), F32),
        compiler_params=_cparams(("arbitrary",)),
        name="out_proj_ln",
    )(x, *hs, *ws, g, b)


def _ffn_kernel(x_ref, wg_ref, wu_ref, wd_ref, g_ref, b_ref, o_ref, acc_ref, xb_ref, *, nf):
    f = pl.program_id(1)

    @pl.when(f == 0)
    def _():
        acc_ref[...] = jnp.zeros_like(acc_ref)
        xb_ref[...] = x_ref[...].astype(BF16)

    xb = xb_ref[...]
    h = _silu(_dot(xb, wg_ref[...])) * _dot(xb, wu_ref[...])
    acc_ref[...] += _dot(h.astype(BF16), wd_ref[...])

    @pl.when(f == nf - 1)
    def _():
        o_ref[...] = _layer_norm(DEEPNORM_ALPHA * x_ref[...] + acc_ref[...], g_ref[...], b_ref[...])


def ffn_ln(x, wg, wu, wd, g, b, tm=512, tf=1408):
    m = x.shape[0]
    tm = min(tm, m)
    assert m % tm == 0 and D_FF % tf == 0
    nf = D_FF // tf
    const = lambda shape: pl.BlockSpec(shape, lambda i, f: (0,) * len(shape))
    return pl.pallas_call(
        functools.partial(_ffn_kernel, nf=nf),
        grid=(m // tm, nf),
        in_specs=[pl.BlockSpec((tm, D_MODEL), lambda i, f: (i, 0)),
                  pl.BlockSpec((D_MODEL, tf), lambda i, f: (0, f)),
                  pl.BlockSpec((D_MODEL, tf), lambda i, f: (0, f)),
                  pl.BlockSpec((tf, D_MODEL), lambda i, f: (f, 0)),
                  const((1, D_MODEL)), const((1, D_MODEL))],
        out_specs=pl.BlockSpec((tm, D_MODEL), lambda i, f: (i, 0)),
        out_shape=jax.ShapeDtypeStruct((m, D_MODEL), F32),
        scratch_shapes=[pltpu.VMEM((tm, D_MODEL), F32), pltpu.VMEM((tm, D_MODEL), BF16)],
        compiler_params=_cparams(("arbitrary", "arbitrary")),
        name="ffn_ln",
    )(x, wg, wu, wd, g, b)


MOE_CHUNK = 128


def _moe_kernel(x_ref, wr_ref, wg_ref, wu_ref, wd_ref, g_ref, b_ref, o_ref,
                acc_ref, gates_t_ref, rank_ref, rank_t_ref, xb_ref, xs_ref, y_ref, wrow_ref, cnt_ref,
                *, ne, nf, tm):
    e = pl.program_id(1)
    f = pl.program_id(2)
    nch = tm // MOE_CHUNK

    @pl.when((e == 0) & (f == 0))
    def _route():
        x = x_ref[...]
        acc_ref[...] = jnp.zeros_like(acc_ref)
        xb_ref[...] = x.astype(BF16)
        lane = lax.broadcasted_iota(jnp.int32, (tm, LANES), 1)
        logits = jnp.dot(x, wr_ref[...], precision=HI, preferred_element_type=F32)
        logits = jnp.where(lane < ne, logits, -jnp.inf)
        v1 = jnp.max(logits, -1, keepdims=True)
        i1 = jnp.min(jnp.where(logits == v1, lane, LANES), -1, keepdims=True)
        rest = jnp.where(lane == i1, -jnp.inf, logits)
        v2 = jnp.max(rest, -1, keepdims=True)
        i2 = jnp.min(jnp.where(rest == v2, lane, LANES), -1, keepdims=True)
        e2 = jnp.exp(v2 - v1)
        w1 = 1.0 / (1.0 + e2)
        gates = jnp.where(lane == i1, w1, 0.0) + jnp.where(lane == i2, e2 * w1, 0.0)
        routed = (lane == i1) | (lane == i2)
        cm = jnp.where(routed, 1.0, 0.0)
        cmb = cm.astype(BF16)
        col = lax.broadcasted_iota(jnp.int32, (MOE_CHUNK, tm), 1)
        row = lax.broadcasted_iota(jnp.int32, (MOE_CHUNK, tm), 0)
        for rc in range(nch):
            earlier = jnp.where(col < row + rc * MOE_CHUNK, 1.0, 0.0).astype(BF16)
            rows = slice(rc * MOE_CHUNK, (rc + 1) * MOE_CHUNK)
            rank_ref[rows, :] = jnp.where(routed[rows], _dot(earlier, cmb), -1.0)
        rank_t_ref[...] = rank_ref[...].T
        gates_t_ref[...] = gates.T
        for ee in range(ne):
            cnt_ref[ee] = jnp.sum(jnp.where(lane == ee, cm, 0.0)).astype(jnp.int32)

    cnt = cnt_ref[e]

    @pl.when(f == 0)
    def _gather():
        r_row = rank_t_ref[pl.ds(e, 1), :]
        g_row = gates_t_ref[pl.ds(e, 1), :]
        slot = lax.broadcasted_iota(jnp.int32, (MOE_CHUNK, tm), 0)
        for c in range(nch):
            @pl.when(c * MOE_CHUNK < cnt)
            def _():
                rows = slice(c * MOE_CHUNK, (c + 1) * MOE_CHUNK)
                hit = r_row == (slot + c * MOE_CHUNK).astype(F32)
                xs_ref[rows, :] = _dot(jnp.where(hit, 1.0, 0.0).astype(BF16), xb_ref[...]).astype(BF16)
                w = jnp.sum(jnp.where(hit, g_row, 0.0), -1, keepdims=True)
                wrow_ref[rows, :] = jnp.broadcast_to(w, (MOE_CHUNK, LANES))

    for c in range(nch):
        @pl.when(c * MOE_CHUNK < cnt)
        def _():
            rows = slice(c * MOE_CHUNK, (c + 1) * MOE_CHUNK)
            xc = xs_ref[rows, :]
            h = _silu(_dot(xc, wg_ref[0])) * _dot(xc, wu_ref[0])
            yv = _dot(h.astype(BF16), wd_ref[0])

            @pl.when(f == 0)
            def _():
                y_ref[rows, :] = yv

            @pl.when(f > 0)
            def _():
                y_ref[rows, :] += yv

    @pl.when(f == nf - 1)
    def _combine():
        lane = lax.broadcasted_iota(jnp.int32, (tm, LANES), 1)
        r_col = jnp.sum(jnp.where(lane == e, rank_ref[...], 0.0), -1, keepdims=True)
        slot = lax.broadcasted_iota(jnp.int32, (tm, MOE_CHUNK), 1)
        for c in range(nch):
            @pl.when(c * MOE_CHUNK < cnt)
            def _():
                rows = slice(c * MOE_CHUNK, (c + 1) * MOE_CHUNK)
                hit_t = jnp.where(r_col == (slot + c * MOE_CHUNK).astype(F32), 1.0, 0.0).astype(BF16)
                ys = (y_ref[rows, :] * wrow_ref[rows, 0:1]).astype(BF16)
                acc_ref[...] += _dot(hit_t, ys)

    @pl.when((e == ne - 1) & (f == nf - 1))
    def _():
        o_ref[...] = _layer_norm(DEEPNORM_ALPHA * x_ref[...] + acc_ref[...], g_ref[...], b_ref[...])


def moe_ln(x, w_router, wg, wu, wd, g, b, tm=1024, tf=1408):
    m = x.shape[0]
    tm = min(tm, m)
    assert m % tm == 0 and D_FF % tf == 0 and tm % MOE_CHUNK == 0
    ne = wg.shape[0]
    nf = D_FF // tf
    const = lambda shape: pl.BlockSpec(shape, lambda i, e, f: (0,) * len(shape))
    return pl.pallas_call(
        functools.partial(_moe_kernel, ne=ne, nf=nf, tm=tm),
        grid=(m // tm, ne, nf),
        in_specs=[pl.BlockSpec((tm, D_MODEL), lambda i, e, f: (i, 0)),
                  const((D_MODEL, LANES)),
                  pl.BlockSpec((1, D_MODEL, tf), lambda i, e, f: (e, 0, f)),
                  pl.BlockSpec((1, D_MODEL, tf), lambda i, e, f: (e, 0, f)),
                  pl.BlockSpec((1, tf, D_MODEL), lambda i, e, f: (e, f, 0)),
                  const((1, D_MODEL)), const((1, D_MODEL))],
        out_specs=pl.BlockSpec((tm, D_MODEL), lambda i, e, f: (i, 0)),
        out_shape=jax.ShapeDtypeStruct((m, D_MODEL), F32),
        scratch_shapes=[pltpu.VMEM((tm, D_MODEL), F32),
                        pltpu.VMEM((LANES, tm), F32),
                        pltpu.VMEM((tm, LANES), F32),
                        pltpu.VMEM((LANES, tm), F32),
                        pltpu.VMEM((tm, D_MODEL), BF16),
                        pltpu.VMEM((tm, D_MODEL), BF16),
                        pltpu.VMEM((tm, D_MODEL), F32),
                        pltpu.VMEM((tm, LANES), F32),
                        pltpu.SMEM((ne,), jnp.int32)],
        compiler_params=_cparams(("arbitrary", "arbitrary", "arbitrary")),
        name="moe_ln",
    )(x, w_router, wg, wu, wd, g, b)


def _ple_kernel(x_ref, p_ref, wg_ref, wp_ref, o_ref):
    x = x_ref[...]
    gate = jax.nn.sigmoid(_dot(x.astype(BF16), wg_ref[...]))
    o_ref[...] = x + gate * _dot(p_ref[...].astype(BF16), wp_ref[...])


def ple(x, p, wg, wp, tm=512):
    m = x.shape[0]
    tm = min(tm, m)
    assert m % tm == 0
    return pl.pallas_call(
        _ple_kernel,
        grid=(m // tm,),
        in_specs=[pl.BlockSpec((tm, D_MODEL), lambda i: (i, 0)),
                  pl.BlockSpec((tm, PLE_DIM), lambda i: (i, 0)),
                  pl.BlockSpec((D_MODEL, D_MODEL), lambda i: (0, 0)),
                  pl.BlockSpec((PLE_DIM, D_MODEL), lambda i: (0, 0))],
        out_specs=pl.BlockSpec((tm, D_MODEL), lambda i: (i, 0)),
        out_shape=jax.ShapeDtypeStruct((m, D_MODEL), F32),
        compiler_params=_cparams(("arbitrary",)),
        name="ple",
    )(x, p, wg, wp)


def _qkv_kernel(x_ref, w_ref, *refs, tm, prompt):
    x = x_ref[...].astype(BF16)
    q = _dot(x, w_ref[:, 0:D_ATTN]) * (ATTN_HEAD_DIM ** -0.5)
    k = _dot(x, w_ref[:, D_ATTN:2 * D_ATTN])
    v = _dot(x, w_ref[:, 2 * D_ATTN:])
    if not prompt:
        q_ref, k_ref, v_ref = refs
        q_ref[...] = q
        k_ref[...] = k
        v_ref[...] = v
        return
    q_ref, k16_ref, kt_ref, vt_ref, vt16_ref, km_ref = refs[2:]
    q_ref[...] = q
    k16_ref[...] = k.astype(BF16)
    kt_ref[0, 0] = k.T
    vt = v.T
    vt_ref[0, 0] = vt
    vt16_ref[0] = vt.astype(BF16)
    for j in range(tm // MOBA_BLOCK):
        km_ref[0, j:j + 1, :] = jnp.mean(k[j * MOBA_BLOCK:(j + 1) * MOBA_BLOCK], 0, keepdims=True)


def qkv_proj(x, w, tm=512):
    m = x.shape[0]
    tm = min(tm, m)
    assert m % tm == 0
    row = pl.BlockSpec((tm, D_ATTN), lambda i: (i, 0))
    return pl.pallas_call(
        functools.partial(_qkv_kernel, tm=tm, prompt=False),
        grid=(m // tm,),
        in_specs=[pl.BlockSpec((tm, D_MODEL), lambda i: (i, 0)),
                  pl.BlockSpec((D_MODEL, 3 * D_ATTN), lambda i: (0, 0))],
        out_specs=[row, row, row],
        out_shape=[jax.ShapeDtypeStruct((m, D_ATTN), F32)] * 3,
        compiler_params=_cparams(("arbitrary",)),
        name="qkv_proj",
    )(x, w)


def qkv_proj_prompt(x, w, kt_all, vt_all, layer, tm=512):
    _, b, _, L = kt_all.shape
    assert L % tm == 0 and tm % MOBA_BLOCK == 0 and x.shape[0] == b * L
    nb = tm // MOBA_BLOCK
    tpb = L // tm
    row = pl.BlockSpec((tm, D_ATTN), lambda i: (i, 0))
    slab = pl.BlockSpec((1, 1, D_ATTN, tm), lambda i: (layer, i // tpb, 0, i % tpb))
    hbm = pl.BlockSpec(memory_space=pl.ANY)
    return pl.pallas_call(
        functools.partial(_qkv_kernel, tm=tm, prompt=True),
        grid=(b * tpb,),
        in_specs=[pl.BlockSpec((tm, D_MODEL), lambda i: (i, 0)),
                  pl.BlockSpec((D_MODEL, 3 * D_ATTN), lambda i: (0, 0)), hbm, hbm],
        out_specs=[row, row, slab, slab, pl.BlockSpec((1, D_ATTN, tm), lambda i: (i // tpb, 0, i % tpb)),
                   pl.BlockSpec((1, nb, D_ATTN), lambda i: (i, 0, 0))],
        out_shape=[jax.ShapeDtypeStruct((b * L, D_ATTN), F32), jax.ShapeDtypeStruct((b * L, D_ATTN), BF16),
                   jax.ShapeDtypeStruct(kt_all.shape, F32), jax.ShapeDtypeStruct(vt_all.shape, F32),
                   jax.ShapeDtypeStruct((b, D_ATTN, L), BF16),
                   jax.ShapeDtypeStruct((b * tpb, nb, D_ATTN), F32)],
        input_output_aliases={2: 2, 3: 3},
        compiler_params=_cparams(("arbitrary",)),
        name="qkv_proj_prompt",
    )(x, w, kt_all, vt_all)


def _select_topk(gate, nidx, n_cand, axis=-1):
    g = jnp.where((nidx >= 0) & (nidx < n_cand), gate, -jnp.inf)
    sel = jnp.zeros(gate.shape, jnp.bool_)
    big = jnp.int32(1 << 20)
    for _ in range(MOBA_TOPK):
        m = jnp.max(g, axis, keepdims=True)
        hit = (g == m) & (m > -jnp.inf)
        first = jnp.min(jnp.where(hit, nidx, big), axis, keepdims=True)
        pick = hit & (nidx == first)
        sel = sel | pick
        g = jnp.where(pick, -jnp.inf, g)
    return sel


def _alibi_slopes_col(rows_per_head, n_rows):
    h = lax.broadcasted_iota(jnp.int32, (n_rows, 1), 0) // rows_per_head
    return jnp.exp2(-8.0 * (h + 1).astype(F32) / ATTN_HEADS)


def _moba_prompt_kernel(q_ref, k_ref, vt_ref, km_ref, ft_ref, o_ref, *, npp):
    pp = pl.program_id(1)
    j = pl.program_id(2)
    tq = MOBA_BLOCK
    nq = 2 * tq
    selr = km_ref.shape[1]
    frow = lax.broadcasted_iota(jnp.int32, (selr, nq), 0)
    is_b = lax.broadcasted_iota(jnp.int32, (selr, nq), 1) >= tq
    nidx = frow - FEAT_SEL0
    key_s = lax.broadcasted_iota(jnp.int32, (MOBA_BLOCK, nq), 0)
    q_l = lax.broadcasted_iota(jnp.int32, (MOBA_BLOCK, nq), 1) % tq
    causal = key_s <= q_l
    sub = lax.broadcasted_iota(jnp.int32, (LANES, tq), 0)

    def block_scores(u, qft, n):
        start = pl.multiple_of(n * MOBA_BLOCK, MOBA_BLOCK)
        kf = jnp.concatenate([k_ref[0, pl.ds(start, MOBA_BLOCK), u * LANES:(u + 1) * LANES], ft_ref[n]], axis=1)
        return _dot(kf, qft), vt_ref[0, u * LANES:(u + 1) * LANES, pl.ds(start, MOBA_BLOCK)]

    qfts, carry0 = [], []
    for u in range(npp):
        qt = q_ref[0, :, u * LANES:(u + 1) * LANES].T
        q2t = jnp.concatenate([jnp.where(sub < ATTN_HEAD_DIM, qt, 0.0),
                               jnp.where(sub < ATTN_HEAD_DIM, 0.0, qt)], axis=1)
        gate = jnp.dot(km_ref[0, :, u * LANES:(u + 1) * LANES], q2t, precision=HI,
                       preferred_element_type=F32)
        sel = _select_topk(gate, nidx, j, axis=0)
        head = 2 * (pp * npp + u) + is_b.astype(jnp.int32)
        slope = jnp.exp2(-8.0 * (head + 1).astype(F32) / ATTN_HEADS) * LOG2E
        s_hi = slope.astype(BF16).astype(F32)
        s_lo = slope - s_hi
        feat = jnp.where(sel | (nidx == j), 0.0, NEG_BIG)
        feat = jnp.where(frow < 2, s_hi, jnp.where(frow < FEAT_SEL0, s_lo, feat))
        qft = jnp.concatenate([q2t * LOG2E, feat, jnp.zeros((LANES - selr, nq), F32)],
                              axis=0).astype(BF16)
        s, vtb = block_scores(u, qft, j)
        s = jnp.where(causal, s, NEG_BIG)
        m0 = jnp.max(s, 0, keepdims=True)
        p0 = jnp.exp2(s - m0)
        qfts.append(qft)
        carry0.append((m0, jnp.sum(p0, 0, keepdims=True), _dot(vtb, p0.astype(BF16))))

    def body(n, carry):
        sv = [block_scores(u, qfts[u], n) for u in range(npp)]
        stats = []
        for u in range(npp):
            m, l, _ = carry[u]
            s = sv[u][0]
            m_new = jnp.maximum(m, jnp.max(s, 0, keepdims=True))
            alpha = jnp.exp2(m - m_new)
            pn = jnp.exp2(s - m_new)
            stats.append((m_new, alpha, alpha * l + jnp.sum(pn, 0, keepdims=True), pn.astype(BF16)))
        pv = [_dot(sv[u][1], stats[u][3]) for u in range(npp)]
        return tuple((stats[u][0], stats[u][2], stats[u][1] * carry[u][2] + pv[u]) for u in range(npp))

    final = lax.fori_loop(0, j, body, tuple(carry0))
    for u in range(npp):
        _, l, acc = final[u]
        ot = acc / l
        o_ref[0, :, u * LANES:(u + 1) * LANES] = jnp.where(sub < ATTN_HEAD_DIM, ot[:, 0:tq], ot[:, tq:]).T


def moba_prompt_attn(q, k16, vt16, kmean, b, L, npp=4):
    assert L % MOBA_BLOCK == 0 and MOBA_BLOCK % Q_BLOCK == 0 and N_PAIRS % npp == 0
    nblk = L // MOBA_BLOCK
    assert FEAT_SEL0 + nblk <= LANES
    q3 = q.reshape(b, L, D_ATTN)
    k3 = k16.reshape(b, L, D_ATTN)
    selr = -(-(FEAT_SEL0 + nblk) // 8) * 8
    km = jnp.pad(kmean, ((0, 0), (FEAT_SEL0, selr - FEAT_SEL0 - nblk), (0, 0)))
    n_i = jnp.arange(nblk, dtype=jnp.int32)[:, None, None]
    key_i = jnp.arange(MOBA_BLOCK, dtype=jnp.int32)[None, :, None]
    ln = jnp.arange(LANES, dtype=jnp.int32)[None, None, :]
    ft = jnp.where((ln == 0) | (ln == 2), n_i * MOBA_BLOCK,
                   jnp.where((ln == 1) | (ln == 3), key_i, (ln == n_i + FEAT_SEL0).astype(jnp.int32)))
    ft = ft.astype(BF16)
    o = pl.pallas_call(
        functools.partial(_moba_prompt_kernel, npp=npp),
        grid=(b, N_PAIRS // npp, nblk),
        in_specs=[pl.BlockSpec((1, MOBA_BLOCK, npp * LANES), lambda i, p, t: (i, t, p)),
                  pl.BlockSpec((1, L, npp * LANES), lambda i, p, t: (i, 0, p)),
                  pl.BlockSpec((1, npp * LANES, L), lambda i, p, t: (i, p, 0)),
                  pl.BlockSpec((1, selr, npp * LANES), lambda i, p, t: (i, 0, p)),
                  pl.BlockSpec((nblk, MOBA_BLOCK, LANES), lambda i, p, t: (0, 0, 0))],
        out_specs=pl.BlockSpec((1, MOBA_BLOCK, npp * LANES), lambda i, p, t: (i, t, p)),
        out_shape=jax.ShapeDtypeStruct((b, L, D_ATTN), F32),
        compiler_params=_cparams(("arbitrary", "arbitrary", "arbitrary")),
        name="moba_prompt",
    )(q3, k3, vt16, km, ft)
    return o.reshape(b * L, D_ATTN)


def _moba_sample_kernel(pt_ref, q_ref, kn_ref, vn_ref, *refs, t, ngroups, gp, past):
    kc_refs = refs[0:gp]
    vc_refs = refs[gp:2 * gp]
    eb_ref, o_ref, qst_ref, qstb_ref, s_ref, p_ref, oacc_ref = refs[2 * gp:]
    ph = pl.program_id(1)
    grp = pl.program_id(2)
    r = ATTN_HEADS * t
    nfull = past // MOBA_BLOCK
    lane_d = lax.broadcasted_iota(jnp.int32, (r, D_ATTN), 1)
    row_d = lax.broadcasted_iota(jnp.int32, (r, D_ATTN), 0)
    own_lanes = lane_d // ATTN_HEAD_DIM == row_d // t

    @pl.when((ph == 0) & (grp == 0))
    def _():
        qst = jnp.where(own_lanes, jnp.concatenate([q_ref[0]] * ATTN_HEADS, axis=0), 0.0)
        qst_ref[...] = qst
        qstb_ref[...] = qst.astype(BF16)

    @pl.when(ph == 0)
    def _():
        for g in range(gp):
            col = pl.multiple_of((grp * gp + g) * PAGE_SIZE, PAGE_SIZE)
            s_ref[:, pl.ds(col, PAGE_SIZE)] = _dot(qstb_ref[...], kc_refs[g][0, 0].astype(BF16))

    @pl.when((ph == 0) & (grp == ngroups - 1))
    def _():
        qst = qst_ref[...]
        lane = lax.broadcasted_iota(jnp.int32, (r, LANES), 1)
        gate = jnp.zeros((r, LANES), F32)
        for n in range(nfull):
            g_n = jnp.sum(s_ref[:, n * MOBA_BLOCK:(n + 1) * MOBA_BLOCK], -1, keepdims=True) * (1.0 / MOBA_BLOCK)
            gate = jnp.where(lane == n, g_n, gate)
        sel = _select_topk(gate, lane, nfull)
        selb = jnp.where(sel, 0.0, NEG_BIG).astype(BF16)
        slope = _alibi_slopes_col(t, r)
        t_row = lax.broadcasted_iota(jnp.int32, (r, 1), 0) % t
        key_pos = lax.broadcasted_iota(jnp.int32, (r, past), 1)
        dist = (past + t_row - key_pos).astype(F32)
        s = s_ref[...] - slope * dist + _dot(selb, eb_ref[...])
        own_i = lax.broadcasted_iota(jnp.int32, (r, t), 1)
        s_own = _dot_nt(qst, kn_ref[0], precision=HI) - slope * (t_row - own_i).astype(F32)
        s_own = jnp.where(own_i <= t_row, s_own, NEG_BIG)
        m = jnp.maximum(jnp.max(s, -1, keepdims=True), jnp.max(s_own, -1, keepdims=True))
        pe = jnp.exp(s - m)
        pe_own = jnp.exp(s_own - m)
        inv = 1.0 / (jnp.sum(pe, -1, keepdims=True) + jnp.sum(pe_own, -1, keepdims=True))
        p_ref[...] = (pe * inv).astype(BF16)
        oacc_ref[...] = jnp.dot(pe_own * inv, vn_ref[0], precision=HI, preferred_element_type=F32)

    @pl.when(ph == 1)
    def _():
        acc = oacc_ref[...]
        for g in range(gp):
            col = pl.multiple_of((grp * gp + g) * PAGE_SIZE, PAGE_SIZE)
            acc = acc + _dot_nt(p_ref[:, pl.ds(col, PAGE_SIZE)], vc_refs[g][0, 0].astype(BF16))
        oacc_ref[...] = acc

    @pl.when((ph == 1) & (grp == ngroups - 1))
    def _():
        oa = jnp.where(own_lanes, oacc_ref[...], 0.0)
        o = oa[0:t]
        for h in range(1, ATTN_HEADS):
            o = o + oa[h * t:(h + 1) * t]
        o_ref[0] = o


def moba_sample_attn(q, kn, vn, cache_k, cache_v, page_table, layer):
    b, t, _ = q.shape
    npages = page_table.shape[1]
    past = npages * PAGE_SIZE
    assert past % MOBA_BLOCK == 0 and past // MOBA_BLOCK <= LANES
    r = ATTN_HEADS * t
    kc = jnp.transpose(cache_k, (0, 1, 3, 4, 2)).reshape(cache_k.shape[0], cache_k.shape[1], D_ATTN, PAGE_SIZE)
    vc = jnp.transpose(cache_v, (0, 1, 3, 4, 2)).reshape(cache_v.shape[0], cache_v.shape[1], D_ATTN, PAGE_SIZE)
    eb = (jnp.arange(LANES, dtype=jnp.int32)[:, None] == jnp.arange(past, dtype=jnp.int32)[None, :] // MOBA_BLOCK)
    eb = eb.astype(BF16)
    gp = SAMPLE_PAGES_PER_STEP if npages % SAMPLE_PAGES_PER_STEP == 0 else 1
    ngroups = npages // gp
    tok = pl.BlockSpec((1, t, D_ATTN), lambda s, ph, grp, pt: (s, 0, 0))

    def k_spec(g):
        return pl.BlockSpec((1, 1, D_ATTN, PAGE_SIZE), lambda s, ph, grp, pt:
                            (layer, pt[s, jnp.where(ph == 0, grp, ngroups - 1) * gp + g], 0, 0))

    def v_spec(g):
        return pl.BlockSpec((1, 1, D_ATTN, PAGE_SIZE), lambda s, ph, grp, pt:
                            (layer, pt[s, jnp.where(ph == 0, 0, grp) * gp + g], 0, 0))

    grid_spec = pltpu.PrefetchScalarGridSpec(
        num_scalar_prefetch=1,
        grid=(b, 2, ngroups),
        in_specs=[tok, tok, tok] + [k_spec(g) for g in range(gp)] + [v_spec(g) for g in range(gp)]
                 + [pl.BlockSpec((LANES, past), lambda s, ph, grp, pt: (0, 0))],
        out_specs=tok,
        scratch_shapes=[pltpu.VMEM((r, D_ATTN), F32), pltpu.VMEM((r, D_ATTN), BF16), pltpu.VMEM((r, past), F32),
                        pltpu.VMEM((r, past), BF16), pltpu.VMEM((r, D_ATTN), F32)],
    )
    return pl.pallas_call(
        functools.partial(_moba_sample_kernel, t=t, ngroups=ngroups, gp=gp, past=past),
        grid_spec=grid_spec,
        out_shape=jax.ShapeDtypeStruct((b, t, D_ATTN), F32),
        compiler_params=_cparams(("arbitrary", "arbitrary", "arbitrary")),
        name="moba_sample",
    )(page_table, q, kn, vn, *([kc] * gp), *([vc] * gp), eb)


def _row(v, width=None):
    v = v.reshape(1, -1).astype(F32)
    if width is not None and v.shape[1] < width:
        v = jnp.pad(v, ((0, 0), (0, width - v.shape[1])))
    return v


def _even_weights(w_in, conv_w, conv_b, dt_bias, a_log, d_skip, norm_g, dw_w, dw_b, ln_g, ln_b, w_out):
    s1 = D_SSM
    s2 = s1 + SSM_CONV_DIM
    s3 = s2 + SSM_HEADS
    w_dt = jnp.pad(w_in[:, s2:s3], ((0, 0), (0, LANES - SSM_HEADS)))
    w_perm = jnp.concatenate([w_in[:, :s2], w_in[:, s3:], w_dt], axis=1).astype(BF16)
    return dict(
        w_in=w_perm, conv_w=conv_w, conv_b=_row(conv_b), dt_bias=_row(dt_bias, LANES), a_log=_row(a_log, LANES),
        d_skip=_row(jnp.repeat(d_skip, SSM_HEAD_DIM)), norm_g=_row(norm_g), dw_w=dw_w, dw_b=_row(dw_b),
        ln_g=_row(ln_g), ln_b=_row(ln_b), w_out_y=w_out[:D_SSM].astype(BF16), w_out_u=w_out[D_SSM:].astype(BF16))


def _even_mixer(x, b, L, conv_buf, ssm_state, conf_buf, ew, mix_g, mix_b):
    zx, u, dtr = proj_even(x, ew["w_in"])
    y, new_state, new_conv = ssd_mixer(zx.reshape(b, L, ZX_W), dtr.reshape(b, L, LANES), conv_buf, ssm_state,
                                       ew["conv_w"], ew["conv_b"], ew["dt_bias"], ew["a_log"], ew["d_skip"],
                                       ew["norm_g"])
    uc, new_conf = conf_mixer(u.reshape(b, L, D_CONF), conf_buf, ew["dw_w"], ew["dw_b"], ew["ln_g"], ew["ln_b"])
    x = out_proj_ln(x, [y.reshape(b * L, D_SSM), uc.reshape(b * L, D_CONF)], [ew["w_out_y"], ew["w_out_u"]],
                    mix_g, mix_b)
    return x, new_state, new_conv, new_conf


def kernel(x_prompt, x_sample, p_prompt, p_sample, state_ssm, state_ssm_conv, state_conf_conv, cache_k, cache_v, page_table, w_in_even, ssm_conv_w, ssm_conv_b, ssm_dt_bias, ssm_a_log, ssm_d, ssm_norm_g, conf_dw_w, conf_dw_b, conf_ln_g, conf_ln_b, w_out_even, w_qkv, w_o, ln_mix_g, ln_mix_b, ln_ffn_g, ln_ffn_b, w_ffn_gate, w_ffn_up, w_ffn_down, w_router, w_exp_gate, w_exp_up, w_exp_down, w_ple, w_ple_gate):
    bp, lp, _ = x_prompt.shape
    bs, ls, _ = x_sample.shape
    xp = x_prompt.reshape(bp * lp, D_MODEL)
    xs = x_sample.reshape(bs * ls, D_MODEL)
    outs = {k: [] for k in ("ssm_p", "sconv_p", "conf_p", "ssm_s", "sconv_s", "conf_s", "k_s", "v_s")}
    kt_all = jnp.zeros((DEPTH // 2, bp, D_ATTN, lp), F32)
    vt_all = jnp.zeros((DEPTH // 2, bp, D_ATTN, lp), F32)
    for i in range(DEPTH):
        li = i // 2
        mix_g, mix_b = _row(ln_mix_g[i]), _row(ln_mix_b[i])
        ffn_g, ffn_b = _row(ln_ffn_g[i]), _row(ln_ffn_b[i])
        if i % 2 == 0:
            ew = _even_weights(w_in_even[li], ssm_conv_w[li], ssm_conv_b[li], ssm_dt_bias[li], ssm_a_log[li],
                               ssm_d[li], ssm_norm_g[li], conf_dw_w[li], conf_dw_b[li], conf_ln_g[li],
                               conf_ln_b[li], w_out_even[li])
            zc = jnp.zeros((bp, SSM_CONV - 1, SSM_CONV_DIM), F32)
            zs = jnp.zeros((bp, SSM_HEADS, SSM_HEAD_DIM, SSM_STATE), F32)
            zf = jnp.zeros((bp, CONF_WIDTH - 1, D_CONF), F32)
            xp, st_p, cv_p, cf_p = _even_mixer(xp, bp, lp, zc, zs, zf, ew, mix_g, mix_b)
            xs, st_s, cv_s, cf_s = _even_mixer(xs, bs, ls, state_ssm_conv[li], state_ssm[li], state_conf_conv[li],
                                               ew, mix_g, mix_b)
            outs["ssm_p"].append(st_p); outs["sconv_p"].append(cv_p); outs["conf_p"].append(cf_p)
            outs["ssm_s"].append(st_s); outs["sconv_s"].append(cv_s); outs["conf_s"].append(cf_s)
            wg = w_ffn_gate[li].astype(BF16)
            wu = w_ffn_up[li].astype(BF16)
            wd = w_ffn_down[li].astype(BF16)
            xp = ffn_ln(xp, wg, wu, wd, ffn_g, ffn_b)
            xs = ffn_ln(xs, wg, wu, wd, ffn_g, ffn_b)
        else:
            wqkv = w_qkv[li].astype(BF16)
            wo = w_o[li].astype(BF16)
            qp, kp16, kt_all, vt_all, vpt16, kmean = qkv_proj_prompt(xp, wqkv, kt_all, vt_all, li)
            qs, ks, vs = qkv_proj(xs, wqkv)
            hp = moba_prompt_attn(qp, kp16, vpt16, kmean.reshape(bp, lp // MOBA_BLOCK, D_ATTN), bp, lp)
            hs = moba_sample_attn(qs.reshape(bs, ls, D_ATTN), ks.reshape(bs, ls, D_ATTN), vs.reshape(bs, ls, D_ATTN),
                                  cache_k, cache_v, page_table, li)
            shp_s = (bs, ls, ATTN_HEADS, ATTN_HEAD_DIM)
            outs["k_s"].append(ks.reshape(shp_s)); outs["v_s"].append(vs.reshape(shp_s))
            xp = out_proj_ln(xp, [hp], [wo], mix_g, mix_b)
            xs = out_proj_ln(xs, [hs.reshape(bs * ls, D_ATTN)], [wo], mix_g, mix_b)
            wr = jnp.pad(w_router[li], ((0, 0), (0, LANES - N_EXPERTS)))
            wg = w_exp_gate[li].astype(BF16)
            wu = w_exp_up[li].astype(BF16)
            wd = w_exp_down[li].astype(BF16)
            xp = moe_ln(xp, wr, wg, wu, wd, ffn_g, ffn_b)
            xs = moe_ln(xs, wr, wg, wu, wd, ffn_g, ffn_b)
        wpg = w_ple_gate[i].astype(BF16)
        wpp = w_ple[i].astype(BF16)
        xp = ple(xp, p_prompt[i].reshape(bp * lp, PLE_DIM), wpg, wpp)
        xs = ple(xs, p_sample[i].reshape(bs * ls, PLE_DIM), wpg, wpp)
    st = lambda k: jnp.stack(outs[k])
    to_cache = lambda a: jnp.transpose(a.reshape(DEPTH // 2, bp, ATTN_HEADS, ATTN_HEAD_DIM, lp), (0, 1, 4, 2, 3))
    return (xp.reshape(bp, lp, D_MODEL), xs.reshape(bs, ls, D_MODEL),
            st("ssm_p"), st("sconv_p"), st("conf_p"), to_cache(kt_all), to_cache(vt_all),
            st("ssm_s"), st("sconv_s"), st("conf_s"), st("k_s"), st("v_s"))
```

```python
import functools
import math

import jax
import jax.numpy as jnp
from jax import lax
from jax.experimental import pallas as pl
from jax.experimental.pallas import tpu as pltpu

F32 = jnp.float32
BF16 = jnp.bfloat16
HI = lax.Precision.HIGHEST

D_MODEL = 1024
DEPTH = 4
PAGE_SIZE = 128
SSM_HEADS = 16
SSM_HEAD_DIM = 64
D_SSM = SSM_HEADS * SSM_HEAD_DIM
SSM_GROUPS = 2
SSM_STATE = 128
SSM_CONV = 4
SSD_CHUNK = 128
SSM_CONV_DIM = D_SSM + 2 * SSM_GROUPS * SSM_STATE
D_CONF = 1024
CONF_WIDTH = 31
ATTN_HEADS = 16
ATTN_HEAD_DIM = 64
D_ATTN = ATTN_HEADS * ATTN_HEAD_DIM
MOBA_BLOCK = 256
MOBA_TOPK = 3
Q_BLOCK = 128
D_FF = 2816
N_EXPERTS = 8
EXPERT_TOPK = 2
PLE_DIM = 256
DEEPNORM_ALPHA = (2.0 * DEPTH) ** 0.25
LN_EPS = 1e-5

LANES = 128
NEG_BIG = -1e30
LOG2E = math.log2(math.e)
VMEM_LIMIT = 56 * 1024 * 1024
N_PAIRS = ATTN_HEADS // 2
FEAT_SEL0 = 4
SAMPLE_PAGES_PER_STEP = 8


def _cparams(sem):
    return pltpu.CompilerParams(dimension_semantics=sem, vmem_limit_bytes=VMEM_LIMIT)


def _silu(x):
    return x * jax.nn.sigmoid(x)


def _layer_norm(x, g, b):
    xc = x - jnp.mean(x, -1, keepdims=True)
    var = jnp.mean(xc * xc, -1, keepdims=True)
    return xc * lax.rsqrt(var + LN_EPS) * g + b


def _dot(a, b):
    return jnp.dot(a, b, preferred_element_type=F32)


def _dot_nt(a, b, precision=None):
    return lax.dot_general(a, b, (((1,), (1,)), ((), ())), precision=precision,
                           preferred_element_type=F32)


def _dot_tn(a, b):
    return lax.dot_general(a, b, (((0,), (0,)), ((), ())), preferred_element_type=F32)


ZX_W = D_SSM + SSM_CONV_DIM


def _proj_even_kernel(x_ref, w_ref, zx_ref, u_ref, dt_ref):
    x = x_ref[...].astype(BF16)
    zx_ref[...] = _dot(x, w_ref[:, 0:ZX_W])
    a = _dot(x, w_ref[:, ZX_W:ZX_W + D_CONF])
    g = _dot(x, w_ref[:, ZX_W + D_CONF:ZX_W + 2 * D_CONF])
    u_ref[...] = a * jax.nn.sigmoid(g)
    dt_ref[...] = _dot(x, w_ref[:, ZX_W + 2 * D_CONF:])


def proj_even(x, w, tm=256):
    m = x.shape[0]
    tm = min(tm, m)
    assert m % tm == 0
    n = w.shape[1]
    return pl.pallas_call(
        _proj_even_kernel,
        grid=(m // tm,),
        in_specs=[pl.BlockSpec((tm, D_MODEL), lambda i: (i, 0)),
                  pl.BlockSpec((D_MODEL, n), lambda i: (0, 0))],
        out_specs=[pl.BlockSpec((tm, ZX_W), lambda i: (i, 0)),
                   pl.BlockSpec((tm, D_CONF), lambda i: (i, 0)),
                   pl.BlockSpec((tm, LANES), lambda i: (i, 0))],
        out_shape=[jax.ShapeDtypeStruct((m, ZX_W), F32),
                   jax.ShapeDtypeStruct((m, D_CONF), F32),
                   jax.ShapeDtypeStruct((m, LANES), F32)],
        compiler_params=_cparams(("arbitrary",)),
        name="proj_even",
    )(x, w)


SSM_PAD = 8
SSM_OFF = SSM_PAD - (SSM_CONV - 1)


def _ssd_kernel(zx_ref, dtr_ref, cbuf_ref, s0_ref, cw_ref, cb_ref, dtb_ref, alog_ref, dsk_ref, ng_ref,
                y_ref, sout_ref, cout_ref, ext_ref, st_ref, ybuf_ref, *, q, nc):
    c = pl.program_id(1)

    @pl.when(c == 0)
    def _():
        ext_ref[0:SSM_PAD, :] = jnp.zeros((SSM_PAD, SSM_CONV_DIM), F32)
        ext_ref[SSM_OFF:SSM_PAD, :] = cbuf_ref[0]
        st_ref[...] = s0_ref[0]

    ext_ref[SSM_PAD:SSM_PAD + q, :] = zx_ref[0, :, D_SSM:ZX_W]
    acc = jnp.broadcast_to(cb_ref[...], (q, SSM_CONV_DIM))
    for k in range(SSM_CONV):
        acc = acc + ext_ref[SSM_OFF + k:SSM_OFF + k + q, :] * cw_ref[k:k + 1, :]
    xbc = _silu(acc)

    @pl.when(c == nc - 1)
    def _():
        cout_ref[0] = ext_ref[q + SSM_OFF:q + SSM_PAD, :]

    ext_ref[0:SSM_PAD, :] = ext_ref[q:q + SSM_PAD, :]

    bm = xbc[:, D_SSM:D_SSM + SSM_GROUPS * SSM_STATE]
    cm = xbc[:, D_SSM + SSM_GROUPS * SSM_STATE:]
    xdt = dtr_ref[0] + dtb_ref[...]
    dt = jnp.maximum(xdt, 0.0) + jnp.log(1.0 + jnp.exp(-jnp.abs(xdt)))
    a = -jnp.exp(alog_ref[...])
    dta = dt * a
    rows = lax.broadcasted_iota(jnp.int32, (q, q), 0)
    cols = lax.broadcasted_iota(jnp.int32, (q, q), 1)
    tril = rows >= cols
    acum = jnp.dot(tril.astype(F32), dta, precision=HI, preferred_element_type=F32)
    acum_t = acum.T
    dt_t = dt.T
    wend = jnp.exp(acum[q - 1:q, :] - acum) * dt
    ea = jnp.exp(acum)
    dec_t = jnp.exp(acum_t[:, q - 1:q])

    bmb = bm.astype(BF16)
    cmb = cm.astype(BF16)
    cb = [_dot_nt(cmb[:, g * SSM_STATE:(g + 1) * SSM_STATE], bmb[:, g * SSM_STATE:(g + 1) * SSM_STATE])
          for g in range(SSM_GROUPS)]
    lane = lax.broadcasted_iota(jnp.int32, (q, LANES), 1)
    is_a = lane < SSM_HEAD_DIM
    sub = lax.broadcasted_iota(jnp.int32, (LANES, LANES), 0)
    hpg = SSM_HEADS // SSM_GROUPS

    def scores(h, g):
        seg = acum[:, h:h + 1] - acum_t[h:h + 1, :]
        dec = jnp.where(tril, jnp.exp(jnp.where(tril, seg, 0.0)), 0.0)
        return (cb[g] * dec * dt_t[h:h + 1, :]).astype(BF16)

    for p in range(SSM_HEADS // 2):
        ha, hb = 2 * p, 2 * p + 1
        g = ha // hpg
        xp = xbc[:, p * LANES:(p + 1) * LANES]
        xa = jnp.where(is_a, xp, 0.0).astype(BF16)
        xb = jnp.where(is_a, 0.0, xp).astype(BF16)
        y = _dot(scores(ha, g), xa) + _dot(scores(hb, g), xb)
        sp = st_ref[p]
        ea_p = jnp.where(is_a, ea[:, ha:ha + 1], ea[:, hb:hb + 1])
        y = y + ea_p * _dot_nt(cmb[:, g * SSM_STATE:(g + 1) * SSM_STATE], sp.astype(BF16))
        y = y + dsk_ref[:, p * LANES:(p + 1) * LANES] * xp
        ybuf_ref[:, p * LANES:(p + 1) * LANES] = y
        wend_p = jnp.where(is_a, wend[:, ha:ha + 1], wend[:, hb:hb + 1])
        upd = _dot_tn((xp * wend_p).astype(BF16), bmb[:, g * SSM_STATE:(g + 1) * SSM_STATE])
        dec_p = jnp.where(sub < SSM_HEAD_DIM, dec_t[ha:ha + 1, :], dec_t[hb:hb + 1, :])
        st_ref[p] = dec_p * sp + upd

    z = zx_ref[0, :, 0:D_SSM]
    yy = ybuf_ref[...] * _silu(z)
    y_ref[0] = yy * lax.rsqrt(jnp.mean(yy * yy, -1, keepdims=True) + LN_EPS) * ng_ref[...]

    @pl.when(c == nc - 1)
    def _():
        sout_ref[0] = st_ref[...]


def ssd_mixer(zx, dtr, cbuf, s0, cw, cb, dtb, alog, dsk, ng):
    b, L, _ = zx.shape
    q = SSD_CHUNK if L % SSD_CHUNK == 0 else L
    nc = L // q
    npair = SSM_HEADS // 2
    s0p = s0.reshape(b, npair, 2 * SSM_HEAD_DIM, SSM_STATE)
    const = lambda shape: pl.BlockSpec(shape, lambda i, j: (0,) * len(shape))
    y, sout, cout = pl.pallas_call(
        functools.partial(_ssd_kernel, q=q, nc=nc),
        grid=(b, nc),
        in_specs=[pl.BlockSpec((1, q, ZX_W), lambda i, j: (i, j, 0)),
                  pl.BlockSpec((1, q, LANES), lambda i, j: (i, j, 0)),
                  pl.BlockSpec((1, SSM_CONV - 1, SSM_CONV_DIM), lambda i, j: (i, 0, 0)),
                  pl.BlockSpec((1, npair, 2 * SSM_HEAD_DIM, SSM_STATE), lambda i, j: (i, 0, 0, 0)),
                  const((SSM_CONV, SSM_CONV_DIM)), const((1, SSM_CONV_DIM)),
                  const((1, LANES)), const((1, LANES)), const((1, D_SSM)), const((1, D_SSM))],
        out_specs=[pl.BlockSpec((1, q, D_SSM), lambda i, j: (i, j, 0)),
                   pl.BlockSpec((1, npair, 2 * SSM_HEAD_DIM, SSM_STATE), lambda i, j: (i, 0, 0, 0)),
                   pl.BlockSpec((1, SSM_CONV - 1, SSM_CONV_DIM), lambda i, j: (i, 0, 0))],
        out_shape=[jax.ShapeDtypeStruct((b, L, D_SSM), F32),
                   jax.ShapeDtypeStruct((b, npair, 2 * SSM_HEAD_DIM, SSM_STATE), F32),
                   jax.ShapeDtypeStruct((b, SSM_CONV - 1, SSM_CONV_DIM), F32)],
        scratch_shapes=[pltpu.VMEM((q + SSM_PAD, SSM_CONV_DIM), F32),
                        pltpu.VMEM((npair, 2 * SSM_HEAD_DIM, SSM_STATE), F32),
                        pltpu.VMEM((q, D_SSM), F32)],
        compiler_params=_cparams(("arbitrary", "arbitrary")),
        name="ssd_mixer",
    )(zx, dtr, cbuf, s0p, cw, cb, dtb, alog, dsk, ng)
    return y, sout.reshape(b, SSM_HEADS, SSM_HEAD_DIM, SSM_STATE), cout


CONF_PAD = 32
CONF_OFF = CONF_PAD - (CONF_WIDTH - 1)
CONF_ROWS = 32


def _conf_kernel(u_ref, buf_ref, w_ref, b_ref, g_ref, beta_ref, o_ref, bout_ref, ext_ref, acc_ref, sh_ref,
                 *, t, nt):
    i = pl.program_id(1)
    sh_rows = t + CONF_PAD - 8

    @pl.when(i == 0)
    def _():
        ext_ref[0:CONF_PAD, :] = jnp.zeros((CONF_PAD, D_CONF), F32)
        ext_ref[CONF_OFF:CONF_PAD, :] = buf_ref[0]

    ext_ref[CONF_PAD:CONF_PAD + t, :] = u_ref[0]
    rr = min(CONF_ROWS, t)

    def chan_body(ci, carry):
        c0 = pl.multiple_of(ci * LANES, LANES)
        for r in range(1, 8):
            sh_ref[r, 0:sh_rows, :] = ext_ref[r:r + sh_rows, pl.ds(c0, LANES)]
        for r0 in range(0, t, rr):
            acc = jnp.broadcast_to(b_ref[:, pl.ds(c0, LANES)], (rr, LANES))
            for k in range(CONF_WIDTH):
                r = (CONF_OFF + k) % 8
                base = r0 + CONF_OFF + k - r
                if r == 0:
                    win = ext_ref[base:base + rr, pl.ds(c0, LANES)]
                else:
                    win = sh_ref[r, base:base + rr, :]
                acc = acc + win * w_ref[k:k + 1, pl.ds(c0, LANES)]
            acc_ref[r0:r0 + rr, pl.ds(c0, LANES)] = acc
        return carry

    lax.fori_loop(0, D_CONF // LANES, chan_body, 0)
    o_ref[0] = _silu(_layer_norm(acc_ref[...], g_ref[...], beta_ref[...]))

    @pl.when(i == nt - 1)
    def _():
        bout_ref[0] = ext_ref[t + CONF_OFF:t + CONF_PAD, :]

    if nt > 1:
        ext_ref[0:CONF_PAD, :] = ext_ref[t:t + CONF_PAD, :]


def conf_mixer(u, buf, w, bias, g, beta, t_max=128):
    b, L, _ = u.shape
    t = t_max if L % t_max == 0 else L
    nt = L // t
    assert nt == 1 or t >= CONF_PAD
    const = lambda shape: pl.BlockSpec(shape, lambda i, j: (0,) * len(shape))
    return pl.pallas_call(
        functools.partial(_conf_kernel, t=t, nt=nt),
        grid=(b, nt),
        in_specs=[pl.BlockSpec((1, t, D_CONF), lambda i, j: (i, j, 0)),
                  pl.BlockSpec((1, CONF_WIDTH - 1, D_CONF), lambda i, j: (i, 0, 0)),
                  const((CONF_WIDTH, D_CONF)), const((1, D_CONF)), const((1, D_CONF)), const((1, D_CONF))],
        out_specs=[pl.BlockSpec((1, t, D_CONF), lambda i, j: (i, j, 0)),
                   pl.BlockSpec((1, CONF_WIDTH - 1, D_CONF), lambda i, j: (i, 0, 0))],
        out_shape=[jax.ShapeDtypeStruct((b, L, D_CONF), F32),
                   jax.ShapeDtypeStruct((b, CONF_WIDTH - 1, D_CONF), F32)],
        scratch_shapes=[pltpu.VMEM((t + CONF_PAD, D_CONF), F32), pltpu.VMEM((t, D_CONF), F32),
                        pltpu.VMEM((8, t + CONF_PAD, LANES), F32)],
        compiler_params=_cparams(("arbitrary", "arbitrary")),
        name="conf_mixer",
    )(u, buf, w, bias, g, beta)


def _out_ln_kernel(*refs, n_in):
    x_ref = refs[0]
    h_refs = refs[1:1 + n_in]
    w_refs = refs[1 + n_in:1 + 2 * n_in]
    g_ref, b_ref, o_ref = refs[1 + 2 * n_in:]
    acc = DEEPNORM_ALPHA * x_ref[...]
    for h_ref, w_ref in zip(h_refs, w_refs):
        acc = acc + _dot(h_ref[...].astype(BF16), w_ref[...])
    o_ref[...] = _layer_norm(acc, g_ref[...], b_ref[...])


def out_proj_ln(x, hs, ws, g, b, tm=512):
    m = x.shape[0]
    tm = min(tm, m)
    assert m % tm == 0
    n_in = len(hs)
    row = lambda k: pl.BlockSpec((tm, k), lambda i: (i, 0))
    const = lambda shape: pl.BlockSpec(shape, lambda i: (0,) * len(shape))
    return pl.pallas_call(
        functools.partial(_out_ln_kernel, n_in=n_in),
        grid=(m // tm,),
        in_specs=[row(D_MODEL)] + [row(h.shape[1]) for h in hs] + [const(w.shape) for w in ws]
                 + [const((1, D_MODEL)), const((1, D_MODEL))],
        out_specs=row(D_MODEL),
        out_shape=jax.ShapeDtypeStruct((m, D_MODEL), F32),
        compiler_params=_cparams(("arbitrary",)),
        name="out_proj_ln",
    )(x, *hs, *ws, g, b)


def _ffn_kernel(x_ref, wg_ref, wu_ref, wd_ref, g_ref, b_ref, o_ref, acc_ref, xb_ref, *, nf):
    f = pl.program_id(1)

    @pl.when(f == 0)
    def _():
        acc_ref[...] = jnp.zeros_like(acc_ref)
        xb_ref[...] = x_ref[...].astype(BF16)

    xb = xb_ref[...]
    h = _silu(_dot(xb, wg_ref[...])) * _dot(xb, wu_ref[...])
    acc_ref[...] += _dot(h.astype(BF16), wd_ref[...])

    @pl.when(f == nf - 1)
    def _():
        o_ref[...] = _layer_norm(DEEPNORM_ALPHA * x_ref[...] + acc_ref[...], g_ref[...], b_ref[...])


def ffn_ln(x, wg, wu, wd, g, b, tm=512, tf=1408):
    m = x.shape[0]
    tm = min(tm, m)
    assert m % tm == 0 and D_FF % tf == 0
    nf = D_FF // tf
    const = lambda shape: pl.BlockSpec(shape, lambda i, f: (0,) * len(shape))
    return pl.pallas_call(
        functools.partial(_ffn_kernel, nf=nf),
        grid=(m // tm, nf),
        in_specs=[pl.BlockSpec((tm, D_MODEL), lambda i, f: (i, 0)),
                  pl.BlockSpec((D_MODEL, tf), lambda i, f: (0, f)),
                  pl.BlockSpec((D_MODEL, tf), lambda i, f: (0, f)),
                  pl.BlockSpec((tf, D_MODEL), lambda i, f: (f, 0)),
                  const((1, D_MODEL)), const((1, D_MODEL))],
        out_specs=pl.BlockSpec((tm, D_MODEL), lambda i, f: (i, 0)),
        out_shape=jax.ShapeDtypeStruct((m, D_MODEL), F32),
        scratch_shapes=[pltpu.VMEM((tm, D_MODEL), F32), pltpu.VMEM((tm, D_MODEL), BF16)],
        compiler_params=_cparams(("arbitrary", "arbitrary")),
        name="ffn_ln",
    )(x, wg, wu, wd, g, b)


MOE_CHUNK = 128


def _moe_kernel(x_ref, wr_ref, wg_ref, wu_ref, wd_ref, g_ref, b_ref, o_ref,
                acc_ref, gates_t_ref, rank_ref, rank_t_ref, xb_ref, xs_ref, y_ref, wrow_ref, cnt_ref,
                *, ne, nf, tm):
    e = pl.program_id(1)
    f = pl.program_id(2)
    nch = tm // MOE_CHUNK

    @pl.when((e == 0) & (f == 0))
    def _route():
        x = x_ref[...]
        acc_ref[...] = jnp.zeros_like(acc_ref)
        xb_ref[...] = x.astype(BF16)
        lane = lax.broadcasted_iota(jnp.int32, (tm, LANES), 1)
        logits = jnp.dot(x, wr_ref[...], precision=HI, preferred_element_type=F32)
        logits = jnp.where(lane < ne, logits, -jnp.inf)
        v1 = jnp.max(logits, -1, keepdims=True)
        i1 = jnp.min(jnp.where(logits == v1, lane, LANES), -1, keepdims=True)
        rest = jnp.where(lane == i1, -jnp.inf, logits)
        v2 = jnp.max(rest, -1, keepdims=True)
        i2 = jnp.min(jnp.where(rest == v2, lane, LANES), -1, keepdims=True)
        e2 = jnp.exp(v2 - v1)
        w1 = 1.0 / (1.0 + e2)
        gates = jnp.where(lane == i1, w1, 0.0) + jnp.where(lane == i2, e2 * w1, 0.0)
        routed = (lane == i1) | (lane == i2)
        cm = jnp.where(routed, 1.0, 0.0)
        cmb = cm.astype(BF16)
        col = lax.broadcasted_iota(jnp.int32, (MOE_CHUNK, tm), 1)
        row = lax.broadcasted_iota(jnp.int32, (MOE_CHUNK, tm), 0)
        for rc in range(nch):
            earlier = jnp.where(col < row + rc * MOE_CHUNK, 1.0, 0.0).astype(BF16)
            rows = slice(rc * MOE_CHUNK, (rc + 1) * MOE_CHUNK)
            rank_ref[rows, :] = jnp.where(routed[rows], _dot(earlier, cmb), -1.0)
        rank_t_ref[...] = rank_ref[...].T
        gates_t_ref[...] = gates.T
        for ee in range(ne):
            cnt_ref[ee] = jnp.sum(jnp.where(lane == ee, cm, 0.0)).astype(jnp.int32)

    cnt = cnt_ref[e]

    @pl.when(f == 0)
    def _gather():
        r_row = rank_t_ref[pl.ds(e, 1), :]
        g_row = gates_t_ref[pl.ds(e, 1), :]
        slot = lax.broadcasted_iota(jnp.int32, (MOE_CHUNK, tm), 0)
        for c in range(nch):
            @pl.when(c * MOE_CHUNK < cnt)
            def _():
                rows = slice(c * MOE_CHUNK, (c + 1) * MOE_CHUNK)
                hit = r_row == (slot + c * MOE_CHUNK).astype(F32)
                xs_ref[rows, :] = _dot(jnp.where(hit, 1.0, 0.0).astype(BF16), xb_ref[...]).astype(BF16)
                w = jnp.sum(jnp.where(hit, g_row, 0.0), -1, keepdims=True)
                wrow_ref[rows, :] = jnp.broadcast_to(w, (MOE_CHUNK, LANES))

    def expert_ffn(r0, nrows):
        rows = slice(r0, r0 + nrows)
        xc = xs_ref[rows, :]
        h = _silu(_dot(xc, wg_ref[0])) * _dot(xc, wu_ref[0])
        yv = _dot(h.astype(BF16), wd_ref[0])

        @pl.when(f == 0)
        def _():
            y_ref[rows, :] = yv

        @pl.when(f > 0)
        def _():
            y_ref[rows, :] += yv

    for c in range(0, nch, 2):
        if c + 1 < nch:
            pl.when((c + 1) * MOE_CHUNK < cnt)(functools.partial(expert_ffn, c * MOE_CHUNK, 2 * MOE_CHUNK))
            pl.when((c * MOE_CHUNK < cnt) & ((c + 1) * MOE_CHUNK >= cnt))(
                functools.partial(expert_ffn, c * MOE_CHUNK, MOE_CHUNK))
        else:
            pl.when(c * MOE_CHUNK < cnt)(functools.partial(expert_ffn, c * MOE_CHUNK, MOE_CHUNK))

    @pl.when(f == nf - 1)
    def _combine():
        lane = lax.broadcasted_iota(jnp.int32, (tm, LANES), 1)
        r_col = jnp.sum(jnp.where(lane == e, rank_ref[...], 0.0), -1, keepdims=True)
        slot = lax.broadcasted_iota(jnp.int32, (tm, MOE_CHUNK), 1)
        for c in range(nch):
            @pl.when(c * MOE_CHUNK < cnt)
            def _():
                rows = slice(c * MOE_CHUNK, (c + 1) * MOE_CHUNK)
                hit_t = jnp.where(r_col == (slot + c * MOE_CHUNK).astype(F32), 1.0, 0.0).astype(BF16)
                ys = (y_ref[rows, :] * wrow_ref[rows, 0:1]).astype(BF16)
                acc_ref[...] += _dot(hit_t, ys)

    @pl.when((e == ne - 1) & (f == nf - 1))
    def _():
        o_ref[...] = _layer_norm(DEEPNORM_ALPHA * x_ref[...] + acc_ref[...], g_ref[...], b_ref[...])


def moe_ln(x, w_router, wg, wu, wd, g, b, tm=1024, tf=1408):
    m = x.shape[0]
    tm = min(tm, m)
    assert m % tm == 0 and D_FF % tf == 0 and tm % MOE_CHUNK == 0
    ne = wg.shape[0]
    nf = D_FF // tf
    const = lambda shape: pl.BlockSpec(shape, lambda i, e, f: (0,) * len(shape))
    return pl.pallas_call(
        functools.partial(_moe_kernel, ne=ne, nf=nf, tm=tm),
        grid=(m // tm, ne, nf),
        in_specs=[pl.BlockSpec((tm, D_MODEL), lambda i, e, f: (i, 0)),
                  const((D_MODEL, LANES)),
                  pl.BlockSpec((1, D_MODEL, tf), lambda i, e, f: (e, 0, f)),
                  pl.BlockSpec((1, D_MODEL, tf), lambda i, e, f: (e, 0, f)),
                  pl.BlockSpec((1, tf, D_MODEL), lambda i, e, f: (e, f, 0)),
                  const((1, D_MODEL)), const((1, D_MODEL))],
        out_specs=pl.BlockSpec((tm, D_MODEL), lambda i, e, f: (i, 0)),
        out_shape=jax.ShapeDtypeStruct((m, D_MODEL), F32),
        scratch_shapes=[pltpu.VMEM((tm, D_MODEL), F32),
                        pltpu.VMEM((LANES, tm), F32),
                        pltpu.VMEM((tm, LANES), F32),
                        pltpu.VMEM((LANES, tm), F32),
                        pltpu.VMEM((tm, D_MODEL), BF16),
                        pltpu.VMEM((tm, D_MODEL), BF16),
                        pltpu.VMEM((tm, D_MODEL), F32),
                        pltpu.VMEM((tm, LANES), F32),
                        pltpu.SMEM((ne,), jnp.int32)],
        compiler_params=_cparams(("arbitrary", "arbitrary", "arbitrary")),
        name="moe_ln",
    )(x, w_router, wg, wu, wd, g, b)


def _ple_kernel(x_ref, p_ref, wg_ref, wp_ref, o_ref):
    x = x_ref[...]
    gate = jax.nn.sigmoid(_dot(x.astype(BF16), wg_ref[...]))
    o_ref[...] = x + gate * _dot(p_ref[...].astype(BF16), wp_ref[...])


def ple(x, p, wg, wp, tm=512):
    m = x.shape[0]
    tm = min(tm, m)
    assert m % tm == 0
    return pl.pallas_call(
        _ple_kernel,
        grid=(m // tm,),
        in_specs=[pl.BlockSpec((tm, D_MODEL), lambda i: (i, 0)),
                  pl.BlockSpec((tm, PLE_DIM), lambda i: (i, 0)),
                  pl.BlockSpec((D_MODEL, D_MODEL), lambda i: (0, 0)),
                  pl.BlockSpec((PLE_DIM, D_MODEL), lambda i: (0, 0))],
        out_specs=pl.BlockSpec((tm, D_MODEL), lambda i: (i, 0)),
        out_shape=jax.ShapeDtypeStruct((m, D_MODEL), F32),
        compiler_params=_cparams(("arbitrary",)),
        name="ple",
    )(x, p, wg, wp)


def _qkv_kernel(x_ref, w_ref, *refs, tm, prompt):
    x = x_ref[...].astype(BF16)
    q = _dot(x, w_ref[:, 0:D_ATTN]) * (ATTN_HEAD_DIM ** -0.5)
    k = _dot(x, w_ref[:, D_ATTN:2 * D_ATTN])
    v = _dot(x, w_ref[:, 2 * D_ATTN:])
    if not prompt:
        q_ref, k_ref, v_ref = refs
        q_ref[...] = q
        k_ref[...] = k
        v_ref[...] = v
        return
    q_ref, k16_ref, kt_ref, vt_ref, vt16_ref, km_ref = refs[2:]
    q_ref[...] = q
    k16_ref[...] = k.astype(BF16)
    kt_ref[0, 0] = k.T
    vt = v.T
    vt_ref[0, 0] = vt
    vt16_ref[0] = vt.astype(BF16)
    for j in range(tm // MOBA_BLOCK):
        km_ref[0, j:j + 1, :] = jnp.mean(k[j * MOBA_BLOCK:(j + 1) * MOBA_BLOCK], 0, keepdims=True)


def qkv_proj(x, w, tm=512):
    m = x.shape[0]
    tm = min(tm, m)
    assert m % tm == 0
    row = pl.BlockSpec((tm, D_ATTN), lambda i: (i, 0))
    return pl.pallas_call(
        functools.partial(_qkv_kernel, tm=tm, prompt=False),
        grid=(m // tm,),
        in_specs=[pl.BlockSpec((tm, D_MODEL), lambda i: (i, 0)),
                  pl.BlockSpec((D_MODEL, 3 * D_ATTN), lambda i: (0, 0))],
        out_specs=[row, row, row],
        out_shape=[jax.ShapeDtypeStruct((m, D_ATTN), F32)] * 3,
        compiler_params=_cparams(("arbitrary",)),
        name="qkv_proj",
    )(x, w)


def qkv_proj_prompt(x, w, kt_all, vt_all, layer, tm=512):
    _, b, _, L = kt_all.shape
    assert L % tm == 0 and tm % MOBA_BLOCK == 0 and x.shape[0] == b * L
    nb = tm // MOBA_BLOCK
    tpb = L // tm
    row = pl.BlockSpec((tm, D_ATTN), lambda i: (i, 0))
    slab = pl.BlockSpec((1, 1, D_ATTN, tm), lambda i: (layer, i // tpb, 0, i % tpb))
    hbm = pl.BlockSpec(memory_space=pl.ANY)
    return pl.pallas_call(
        functools.partial(_qkv_kernel, tm=tm, prompt=True),
        grid=(b * tpb,),
        in_specs=[pl.BlockSpec((tm, D_MODEL), lambda i: (i, 0)),
                  pl.BlockSpec((D_MODEL, 3 * D_ATTN), lambda i: (0, 0)), hbm, hbm],
        out_specs=[row, row, slab, slab, pl.BlockSpec((1, D_ATTN, tm), lambda i: (i // tpb, 0, i % tpb)),
                   pl.BlockSpec((1, nb, D_ATTN), lambda i: (i, 0, 0))],
        out_shape=[jax.ShapeDtypeStruct((b * L, D_ATTN), F32), jax.ShapeDtypeStruct((b * L, D_ATTN), BF16),
                   jax.ShapeDtypeStruct(kt_all.shape, F32), jax.ShapeDtypeStruct(vt_all.shape, F32),
                   jax.ShapeDtypeStruct((b, D_ATTN, L), BF16),
                   jax.ShapeDtypeStruct((b * tpb, nb, D_ATTN), F32)],
        input_output_aliases={2: 2, 3: 3},
        compiler_params=_cparams(("arbitrary",)),
        name="qkv_proj_prompt",
    )(x, w, kt_all, vt_all)


def _select_topk(gate, nidx, n_cand, axis=-1):
    g = jnp.where((nidx >= 0) & (nidx < n_cand), gate, -jnp.inf)
    sel = jnp.zeros(gate.shape, jnp.bool_)
    big = jnp.int32(1 << 20)
    for _ in range(MOBA_TOPK):
        m = jnp.max(g, axis, keepdims=True)
        hit = (g == m) & (m > -jnp.inf)
        first = jnp.min(jnp.where(hit, nidx, big), axis, keepdims=True)
        pick = hit & (nidx == first)
        sel = sel | pick
        g = jnp.where(pick, -jnp.inf, g)
    return sel


def _alibi_slopes_col(rows_per_head, n_rows):
    h = lax.broadcasted_iota(jnp.int32, (n_rows, 1), 0) // rows_per_head
    return jnp.exp2(-8.0 * (h + 1).astype(F32) / ATTN_HEADS)


def _moba_prompt_kernel(q_ref, k_ref, vt_ref, km_ref, ft_ref, o_ref, *, npp):
    pp = pl.program_id(1)
    j = pl.program_id(2)
    tq = MOBA_BLOCK
    nq = 2 * tq
    selr = km_ref.shape[1]
    frow = lax.broadcasted_iota(jnp.int32, (selr, nq), 0)
    is_b = lax.broadcasted_iota(jnp.int32, (selr, nq), 1) >= tq
    nidx = frow - FEAT_SEL0
    key_s = lax.broadcasted_iota(jnp.int32, (MOBA_BLOCK, nq), 0)
    q_l = lax.broadcasted_iota(jnp.int32, (MOBA_BLOCK, nq), 1) % tq
    causal = key_s <= q_l
    sub = lax.broadcasted_iota(jnp.int32, (LANES, tq), 0)

    def block_scores(u, qft, n):
        start = pl.multiple_of(n * MOBA_BLOCK, MOBA_BLOCK)
        kf = jnp.concatenate([k_ref[0, pl.ds(start, MOBA_BLOCK), u * LANES:(u + 1) * LANES], ft_ref[n]], axis=1)
        return _dot(kf, qft), vt_ref[0, u * LANES:(u + 1) * LANES, pl.ds(start, MOBA_BLOCK)]

    qfts, carry0 = [], []
    for u in range(npp):
        qt = q_ref[0, :, u * LANES:(u + 1) * LANES].T
        q2t = jnp.concatenate([jnp.where(sub < ATTN_HEAD_DIM, qt, 0.0),
                               jnp.where(sub < ATTN_HEAD_DIM, 0.0, qt)], axis=1)
        gate = jnp.dot(km_ref[0, :, u * LANES:(u + 1) * LANES], q2t, precision=HI,
                       preferred_element_type=F32)
        sel = _select_topk(gate, nidx, j, axis=0)
        head = 2 * (pp * npp + u) + is_b.astype(jnp.int32)
        slope = jnp.exp2(-8.0 * (head + 1).astype(F32) / ATTN_HEADS) * LOG2E
        s_hi = slope.astype(BF16).astype(F32)
        s_lo = slope - s_hi
        feat = jnp.where(sel | (nidx == j), 0.0, NEG_BIG)
        feat = jnp.where(frow < 2, s_hi, jnp.where(frow < FEAT_SEL0, s_lo, feat))
        qft = jnp.concatenate([q2t * LOG2E, feat, jnp.zeros((LANES - selr, nq), F32)],
                              axis=0).astype(BF16)
        s, vtb = block_scores(u, qft, j)
        s = jnp.where(causal, s, NEG_BIG)
        m0 = jnp.max(s, 0, keepdims=True)
        p0 = jnp.exp2(s - m0)
        qfts.append(qft)
        carry0.append((m0, jnp.sum(p0, 0, keepdims=True), _dot(vtb, p0.astype(BF16))))

    def body(n, carry):
        sv = [block_scores(u, qfts[u], n) for u in range(npp)]
        stats = []
        for u in range(npp):
            m, l, _ = carry[u]
            s = sv[u][0]
            m_new = jnp.maximum(m, jnp.max(s, 0, keepdims=True))
            alpha = jnp.exp2(m - m_new)
            pn = jnp.exp2(s - m_new)
            stats.append((m_new, alpha, alpha * l + jnp.sum(pn, 0, keepdims=True), pn.astype(BF16)))
        pv = [_dot(sv[u][1], stats[u][3]) for u in range(npp)]
        return tuple((stats[u][0], stats[u][2], stats[u][1] * carry[u][2] + pv[u]) for u in range(npp))

    final = lax.fori_loop(0, j, body, tuple(carry0))
    for u in range(npp):
        _, l, acc = final[u]
        ot = acc / l
        o_ref[0, :, u * LANES:(u + 1) * LANES] = jnp.where(sub < ATTN_HEAD_DIM, ot[:, 0:tq], ot[:, tq:]).T


def moba_prompt_attn(q, k16, vt16, kmean, b, L, npp=4):
    assert L % MOBA_BLOCK == 0 and MOBA_BLOCK % Q_BLOCK == 0 and N_PAIRS % npp == 0
    nblk = L // MOBA_BLOCK
    assert FEAT_SEL0 + nblk <= LANES
    q3 = q.reshape(b, L, D_ATTN)
    k3 = k16.reshape(b, L, D_ATTN)
    selr = -(-(FEAT_SEL0 + nblk) // 8) * 8
    km = jnp.pad(kmean, ((0, 0), (FEAT_SEL0, selr - FEAT_SEL0 - nblk), (0, 0)))
    n_i = jnp.arange(nblk, dtype=jnp.int32)[:, None, None]
    key_i = jnp.arange(MOBA_BLOCK, dtype=jnp.int32)[None, :, None]
    ln = jnp.arange(LANES, dtype=jnp.int32)[None, None, :]
    ft = jnp.where((ln == 0) | (ln == 2), n_i * MOBA_BLOCK,
                   jnp.where((ln == 1) | (ln == 3), key_i, (ln == n_i + FEAT_SEL0).astype(jnp.int32)))
    ft = ft.astype(BF16)
    o = pl.pallas_call(
        functools.partial(_moba_prompt_kernel, npp=npp),
        grid=(b, N_PAIRS // npp, nblk),
        in_specs=[pl.BlockSpec((1, MOBA_BLOCK, npp * LANES), lambda i, p, t: (i, t, p)),
                  pl.BlockSpec((1, L, npp * LANES), lambda i, p, t: (i, 0, p)),
                  pl.BlockSpec((1, npp * LANES, L), lambda i, p, t: (i, p, 0)),
                  pl.BlockSpec((1, selr, npp * LANES), lambda i, p, t: (i, 0, p)),
                  pl.BlockSpec((nblk, MOBA_BLOCK, LANES), lambda i, p, t: (0, 0, 0))],
        out_specs=pl.BlockSpec((1, MOBA_BLOCK, npp * LANES), lambda i, p, t: (i, t, p)),
        out_shape=jax.ShapeDtypeStruct((b, L, D_ATTN), F32),
        compiler_params=_cparams(("arbitrary", "arbitrary", "arbitrary")),
        name="moba_prompt",
    )(q3, k3, vt16, km, ft)
    return o.reshape(b * L, D_ATTN)


def _moba_sample_kernel(pt_ref, q_ref, kn_ref, vn_ref, *refs, t, ngroups, gp, past):
    kc_refs = refs[0:gp]
    vc_refs = refs[gp:2 * gp]
    eb_ref, o_ref, qst_ref, qstb_ref, s_ref, p_ref, oacc_ref = refs[2 * gp:]
    ph = pl.program_id(1)
    grp = pl.program_id(2)
    pw = 2 if gp % 2 == 0 else 1
    r = ATTN_HEADS * t
    nfull = past // MOBA_BLOCK
    lane_d = lax.broadcasted_iota(jnp.int32, (r, D_ATTN), 1)
    row_d = lax.broadcasted_iota(jnp.int32, (r, D_ATTN), 0)
    own_lanes = lane_d // ATTN_HEAD_DIM == row_d // t

    @pl.when((ph == 0) & (grp == 0))
    def _():
        qst = jnp.where(own_lanes, jnp.concatenate([q_ref[0]] * ATTN_HEADS, axis=0), 0.0)
        qst_ref[...] = qst
        qstb_ref[...] = qst.astype(BF16)

    @pl.when(ph == 0)
    def _():
        for g in range(0, gp, pw):
            col = pl.multiple_of((grp * gp + g) * PAGE_SIZE, pw * PAGE_SIZE)
            kt = jnp.concatenate([kc_refs[g + i][0, 0].astype(BF16) for i in range(pw)], axis=1)
            s_ref[:, pl.ds(col, pw * PAGE_SIZE)] = _dot(qstb_ref[...], kt)

    @pl.when((ph == 0) & (grp == ngroups - 1))
    def _():
        qst = qst_ref[...]
        lane = lax.broadcasted_iota(jnp.int32, (r, LANES), 1)
        gate = jnp.zeros((r, LANES), F32)
        for n in range(nfull):
            g_n = jnp.sum(s_ref[:, n * MOBA_BLOCK:(n + 1) * MOBA_BLOCK], -1, keepdims=True) * (1.0 / MOBA_BLOCK)
            gate = jnp.where(lane == n, g_n, gate)
        sel = _select_topk(gate, lane, nfull)
        selb = jnp.where(sel, 0.0, NEG_BIG).astype(BF16)
        slope = _alibi_slopes_col(t, r)
        t_row = lax.broadcasted_iota(jnp.int32, (r, 1), 0) % t
        key_pos = lax.broadcasted_iota(jnp.int32, (r, past), 1)
        dist = (past + t_row - key_pos).astype(F32)
        s = s_ref[...] - slope * dist + _dot(selb, eb_ref[...])
        own_i = lax.broadcasted_iota(jnp.int32, (r, t), 1)
        s_own = _dot_nt(qst, kn_ref[0], precision=HI) - slope * (t_row - own_i).astype(F32)
        s_own = jnp.where(own_i <= t_row, s_own, NEG_BIG)
        m = jnp.maximum(jnp.max(s, -1, keepdims=True), jnp.max(s_own, -1, keepdims=True))
        pe = jnp.exp(s - m)
        pe_own = jnp.exp(s_own - m)
        inv = 1.0 / (jnp.sum(pe, -1, keepdims=True) + jnp.sum(pe_own, -1, keepdims=True))
        p_ref[...] = (pe * inv).astype(BF16)
        oacc_ref[...] = jnp.dot(pe_own * inv, vn_ref[0], precision=HI, preferred_element_type=F32)

    @pl.when(ph == 1)
    def _():
        acc = oacc_ref[...]
        for g in range(0, gp, pw):
            col = pl.multiple_of((grp * gp + g) * PAGE_SIZE, pw * PAGE_SIZE)
            vt = jnp.concatenate([vc_refs[g + i][0, 0].astype(BF16) for i in range(pw)], axis=1)
            acc = acc + _dot_nt(p_ref[:, pl.ds(col, pw * PAGE_SIZE)], vt)
        oacc_ref[...] = acc

    @pl.when((ph == 1) & (grp == ngroups - 1))
    def _():
        oa = jnp.where(own_lanes, oacc_ref[...], 0.0)
        o = oa[0:t]
        for h in range(1, ATTN_HEADS):
            o = o + oa[h * t:(h + 1) * t]
        o_ref[0] = o


def moba_sample_attn(q, kn, vn, cache_k, cache_v, page_table, layer):
    b, t, _ = q.shape
    npages = page_table.shape[1]
    past = npages * PAGE_SIZE
    assert past % MOBA_BLOCK == 0 and past // MOBA_BLOCK <= LANES
    r = ATTN_HEADS * t
    kc = jnp.transpose(cache_k, (0, 1, 3, 4, 2)).reshape(cache_k.shape[0], cache_k.shape[1], D_ATTN, PAGE_SIZE)
    vc = jnp.transpose(cache_v, (0, 1, 3, 4, 2)).reshape(cache_v.shape[0], cache_v.shape[1], D_ATTN, PAGE_SIZE)
    eb = (jnp.arange(LANES, dtype=jnp.int32)[:, None] == jnp.arange(past, dtype=jnp.int32)[None, :] // MOBA_BLOCK)
    eb = eb.astype(BF16)
    gp = SAMPLE_PAGES_PER_STEP if npages % SAMPLE_PAGES_PER_STEP == 0 else 1
    ngroups = npages // gp
    tok = pl.BlockSpec((1, t, D_ATTN), lambda s, ph, grp, pt: (s, 0, 0))

    def k_spec(g):
        return pl.BlockSpec((1, 1, D_ATTN, PAGE_SIZE), lambda s, ph, grp, pt:
                            (layer, pt[s, jnp.where(ph == 0, grp, ngroups - 1) * gp + g], 0, 0))

    def v_spec(g):
        return pl.BlockSpec((1, 1, D_ATTN, PAGE_SIZE), lambda s, ph, grp, pt:
                            (layer, pt[s, jnp.where(ph == 0, 0, grp) * gp + g], 0, 0))

    grid_spec = pltpu.PrefetchScalarGridSpec(
        num_scalar_prefetch=1,
        grid=(b, 2, ngroups),
        in_specs=[tok, tok, tok] + [k_spec(g) for g in range(gp)] + [v_spec(g) for g in range(gp)]
                 + [pl.BlockSpec((LANES, past), lambda s, ph, grp, pt: (0, 0))],
        out_specs=tok,
        scratch_shapes=[pltpu.VMEM((r, D_ATTN), F32), pltpu.VMEM((r, D_ATTN), BF16), pltpu.VMEM((r, past), F32),
                        pltpu.VMEM((r, past), BF16), pltpu.VMEM((r, D_ATTN), F32)],
    )
    return pl.pallas_call(
        functools.partial(_moba_sample_kernel, t=t, ngroups=ngroups, gp=gp, past=past),
        grid_spec=grid_spec,
        out_shape=jax.ShapeDtypeStruct((b, t, D_ATTN), F32),
        compiler_params=_cparams(("arbitrary", "arbitrary", "arbitrary")),
        name="moba_sample",
    )(page_table, q, kn, vn, *([kc] * gp), *([vc] * gp), eb)


def _row(v, width=None):
    v = v.reshape(1, -1).astype(F32)
    if width is not None and v.shape[1] < width:
        v = jnp.pad(v, ((0, 0), (0, width - v.shape[1])))
    return v


def _even_weights(w_in, conv_w, conv_b, dt_bias, a_log, d_skip, norm_g, dw_w, dw_b, ln_g, ln_b, w_out):
    s1 = D_SSM
    s2 = s1 + SSM_CONV_DIM
    s3 = s2 + SSM_HEADS
    w_dt = jnp.pad(w_in[:, s2:s3], ((0, 0), (0, LANES - SSM_HEADS)))
    w_perm = jnp.concatenate([w_in[:, :s2], w_in[:, s3:], w_dt], axis=1).astype(BF16)
    return dict(
        w_in=w_perm, conv_w=conv_w, conv_b=_row(conv_b), dt_bias=_row(dt_bias, LANES), a_log=_row(a_log, LANES),
        d_skip=_row(jnp.repeat(d_skip, SSM_HEAD_DIM)), norm_g=_row(norm_g), dw_w=dw_w, dw_b=_row(dw_b),
        ln_g=_row(ln_g), ln_b=_row(ln_b), w_out_y=w_out[:D_SSM].astype(BF16), w_out_u=w_out[D_SSM:].astype(BF16))


def _even_mixer(x, b, L, conv_buf, ssm_state, conf_buf, ew, mix_g, mix_b):
    zx, u, dtr = proj_even(x, ew["w_in"])
    y, new_state, new_conv = ssd_mixer(zx.reshape(b, L, ZX_W), dtr.reshape(b, L, LANES), conv_buf, ssm_state,
                                       ew["conv_w"], ew["conv_b"], ew["dt_bias"], ew["a_log"], ew["d_skip"],
                                       ew["norm_g"])
    uc, new_conf = conf_mixer(u.reshape(b, L, D_CONF), conf_buf, ew["dw_w"], ew["dw_b"], ew["ln_g"], ew["ln_b"])
    x = out_proj_ln(x, [y.reshape(b * L, D_SSM), uc.reshape(b * L, D_CONF)], [ew["w_out_y"], ew["w_out_u"]],
                    mix_g, mix_b)
    return x, new_state, new_conv, new_conf


def kernel(x_prompt, x_sample, p_prompt, p_sample, state_ssm, state_ssm_conv, state_conf_conv, cache_k, cache_v, page_table, w_in_even, ssm_conv_w, ssm_conv_b, ssm_dt_bias, ssm_a_log, ssm_d, ssm_norm_g, conf_dw_w, conf_dw_b, conf_ln_g, conf_ln_b, w_out_even, w_qkv, w_o, ln_mix_g, ln_mix_b, ln_ffn_g, ln_ffn_b, w_ffn_gate, w_ffn_up, w_ffn_down, w_router, w_exp_gate, w_exp_up, w_exp_down, w_ple, w_ple_gate):
    bp, lp, _ = x_prompt.shape
    bs, ls, _ = x_sample.shape
    xp = x_prompt.reshape(bp * lp, D_MODEL)
    xs = x_sample.reshape(bs * ls, D_MODEL)
    outs = {k: [] for k in ("ssm_p", "sconv_p", "conf_p", "ssm_s", "sconv_s", "conf_s", "k_s", "v_s")}
    kt_all = jnp.zeros((DEPTH // 2, bp, D_ATTN, lp), F32)
    vt_all = jnp.zeros((DEPTH // 2, bp, D_ATTN, lp), F32)
    for i in range(DEPTH):
        li = i // 2
        mix_g, mix_b = _row(ln_mix_g[i]), _row(ln_mix_b[i])
        ffn_g, ffn_b = _row(ln_ffn_g[i]), _row(ln_ffn_b[i])
        if i % 2 == 0:
            ew = _even_weights(w_in_even[li], ssm_conv_w[li], ssm_conv_b[li], ssm_dt_bias[li], ssm_a_log[li],
                               ssm_d[li], ssm_norm_g[li], conf_dw_w[li], conf_dw_b[li], conf_ln_g[li],
                               conf_ln_b[li], w_out_even[li])
            zc = jnp.zeros((bp, SSM_CONV - 1, SSM_CONV_DIM), F32)
            zs = jnp.zeros((bp, SSM_HEADS, SSM_HEAD_DIM, SSM_STATE), F32)
            zf = jnp.zeros((bp, CONF_WIDTH - 1, D_CONF), F32)
            xp, st_p, cv_p, cf_p = _even_mixer(xp, bp, lp, zc, zs, zf, ew, mix_g, mix_b)
            xs, st_s, cv_s, cf_s = _even_mixer(xs, bs, ls, state_ssm_conv[li], state_ssm[li], state_conf_conv[li],
                                               ew, mix_g, mix_b)
            outs["ssm_p"].append(st_p); outs["sconv_p"].append(cv_p); outs["conf_p"].append(cf_p)
            outs["ssm_s"].append(st_s); outs["sconv_s"].append(cv_s); outs["conf_s"].append(cf_s)
            wg = w_ffn_gate[li].astype(BF16)
            wu = w_ffn_up[li].astype(BF16)
            wd = w_ffn_down[li].astype(BF16)
            xp = ffn_ln(xp, wg, wu, wd, ffn_g, ffn_b)
            xs = ffn_ln(xs, wg, wu, wd, ffn_g, ffn_b)
        else:
            wqkv = w_qkv[li].astype(BF16)
            wo = w_o[li].astype(BF16)
            qp, kp16, kt_all, vt_all, vpt16, kmean = qkv_proj_prompt(xp, wqkv, kt_all, vt_all, li)
            qs, ks, vs = qkv_proj(xs, wqkv)
            hp = moba_prompt_attn(qp, kp16, vpt16, kmean.reshape(bp, lp // MOBA_BLOCK, D_ATTN), bp, lp)
            hs = moba_sample_attn(qs.reshape(bs, ls, D_ATTN), ks.reshape(bs, ls, D_ATTN), vs.reshape(bs, ls, D_ATTN),
                                  cache_k, cache_v, page_table, li)
            shp_s = (bs, ls, ATTN_HEADS, ATTN_HEAD_DIM)
            outs["k_s"].append(ks.reshape(shp_s)); outs["v_s"].append(vs.reshape(shp_s))
            xp = out_proj_ln(xp, [hp], [wo], mix_g, mix_b)
            xs = out_proj_ln(xs, [hs.reshape(bs * ls, D_ATTN)], [wo], mix_g, mix_b)
            wr = jnp.pad(w_router[li], ((0, 0), (0, LANES - N_EXPERTS)))
            wg = w_exp_gate[li].astype(BF16)
            wu = w_exp_up[li].astype(BF16)
            wd = w_exp_down[li].astype(BF16)
            xp = moe_ln(xp, wr, wg, wu, wd, ffn_g, ffn_b)
            xs = moe_ln(xs, wr, wg, wu, wd, ffn_g, ffn_b)
        wpg = w_ple_gate[i].astype(BF16)
        wpp = w_ple[i].astype(BF16)
        xp = ple(xp, p_prompt[i].reshape(bp * lp, PLE_DIM), wpg, wpp)
        xs = ple(xs, p_sample[i].reshape(bs * ls, PLE_DIM), wpg, wpp)
    st = lambda k: jnp.stack(outs[k])
    to_cache = lambda a: jnp.transpose(a.reshape(DEPTH // 2, bp, ATTN_HEADS, ATTN_HEAD_DIM, lp), (0, 1, 4, 2, 3))
    return (xp.reshape(bp, lp, D_MODEL), xs.reshape(bs, ls, D_MODEL),
            st("ssm_p"), st("sconv_p"), st("conf_p"), to_cache(kt_all), to_cache(vt_all),
            st("ssm_s"), st("sconv_s"), st("conf_s"), st("k_s"), st("v_s"))
```
